```python
import jax, jax.numpy as jnp
from jax import lax
import numpy as np

D_MODEL = 1024
BATCH = 16
SEQ = 2048
DEPTH = 2

CHUNK = 64
Q_BLOCK = 128
LN_EPS = 1e-5
DEEPNORM_ALPHA = (2 * DEPTH) ** 0.25
DEEPNORM_BETA = (8 * DEPTH) ** -0.25

MLSTM_HEADS = 4
MLSTM_DH = 128
MLSTM_W = MLSTM_HEADS * MLSTM_DH
LRU_W = 512
LRU_BLOCKS = 8
LRU_BD = LRU_W // LRU_BLOCKS
CONV_W = 4
LRU_C = 8.0
RET_HEADS = 4
RET_DK = 128
RET_DV = 128
RET_W = RET_HEADS * RET_DV
ROPE_BASE = 10000.0
FOX_HEADS = 4
FOX_DH = 128
FOX_W = FOX_HEADS * FOX_DH
N_BRANCH = 4
BRANCH_W = 512
D_FF = 4 * D_MODEL

SEGMENTS = (
    ("m_q", MLSTM_W), ("m_k", MLSTM_W), ("m_v", MLSTM_W), ("m_o", MLSTM_W),
    ("m_i", MLSTM_HEADS), ("m_f", MLSTM_HEADS),
    ("l_x", LRU_W), ("l_g", LRU_W),
    ("r_q", RET_HEADS * RET_DK), ("r_k", RET_HEADS * RET_DK), ("r_v", RET_W), ("r_g", RET_W),
    ("f_q", FOX_W), ("f_k", FOX_W), ("f_v", FOX_W), ("f_f", FOX_HEADS),
    ("gate", N_BRANCH * D_MODEL),
)
N_IN = sum(w for _, w in SEGMENTS)

kernel_name = "chunk_causal_hybrid_mlstm_rglru_retention_fox"


def _layernorm(x):
    xf = x.astype(jnp.float32)
    mu = jnp.mean(xf, axis=-1, keepdims=True)
    var = jnp.mean(jnp.square(xf - mu), axis=-1, keepdims=True)
    return ((xf - mu) * lax.rsqrt(var + LN_EPS)).astype(x.dtype)


def _head_norm(h, w):
    B, S, H, d = h.shape
    mu = jnp.mean(h, axis=-1, keepdims=True)
    var = jnp.mean(jnp.square(h - mu), axis=-1, keepdims=True)
    return ((h - mu) * lax.rsqrt(var + LN_EPS)).reshape(B, S, H * d) * w


def _split_combined(z):
    idx, acc = [], 0
    for _, w in SEGMENTS[:-1]:
        acc += w
        idx.append(acc)
    parts = jnp.split(z, idx, axis=-1)
    return {name: p for (name, _), p in zip(SEGMENTS, parts)}


def _to_chunks(t):
    B, S, H, d = t.shape
    return t.reshape(B, S // CHUNK, CHUNK, H, d).transpose(1, 0, 3, 2, 4)


def _from_chunks(t):
    nc, B, H, L, d = t.shape
    return t.transpose(1, 0, 3, 2, 4).reshape(B, nc * L, H, d)


def _gates_to_chunks(g):
    B, S, H = g.shape
    return g.reshape(B, S // CHUNK, CHUNK, H).transpose(1, 0, 3, 2)


def _mlstm(q, k, v, o_pre, i_pre, f_pre, norm_w):
    f32 = jnp.float32
    B, S, _ = q.shape
    H, dh = MLSTM_HEADS, MLSTM_DH
    qc = _to_chunks(q.astype(f32).reshape(B, S, H, dh))
    kc = _to_chunks(k.astype(f32).reshape(B, S, H, dh)) * dh ** -0.5
    vc = _to_chunks(v.astype(f32).reshape(B, S, H, dh))
    log_f = _gates_to_chunks(jax.nn.log_sigmoid(f_pre.astype(f32)))
    log_i = _gates_to_chunks(i_pre.astype(f32))
    causal = jnp.tril(jnp.ones((CHUNK, CHUNK), dtype=bool))

    def step(carry, inp):
        C, n, m = carry
        q_, k_, v_, lf, li = inp
        b = jnp.cumsum(lf, axis=-1)
        g = b[..., -1]
        d = jnp.where(causal, b[..., :, None] - b[..., None, :] + li[..., None, :], -jnp.inf)
        inter = b + m[..., None]
        m_t = jnp.maximum(inter, jnp.max(d, axis=-1))
        a_inter = jnp.exp(inter - m_t)
        s = jnp.einsum('bhtd,bhsd->bhts', q_, k_) * jnp.exp(d - m_t[..., None])
        num = a_inter[..., None] * jnp.einsum('bhtd,bhde->bhte', q_, C) + jnp.einsum('bhts,bhse->bhte', s, v_)
        den = a_inter * jnp.einsum('bhtd,bhd->bht', q_, n) + jnp.sum(s, axis=-1)
        h = num / jnp.maximum(jnp.abs(den), jnp.exp(-m_t))[..., None]
        ds = g[..., None] - b + li
        m_new = jnp.maximum(g + m, jnp.max(ds, axis=-1))
        ws = jnp.exp(ds - m_new[..., None])
        decay = jnp.exp(g + m - m_new)
        kw = k_ * ws[..., None]
        C_new = decay[..., None, None] * C + jnp.einsum('bhsd,bhse->bhde', kw, v_)
        n_new = decay[..., None] * n + jnp.sum(kw, axis=2)
        return (C_new, n_new, m_new), h

    init = (jnp.zeros((B, H, dh, dh), f32), jnp.zeros((B, H, dh), f32), jnp.zeros((B, H), f32))
    _, hc = lax.scan(step, init, (qc, kc, vc, log_f, log_i))
    o = jax.nn.sigmoid(o_pre.astype(f32)).reshape(B, S, H, dh)
    return _head_norm(o * _from_chunks(hc), norm_w)


def _causal_depthwise_conv(x, w, b):
    C = x.shape[-1]
    y = lax.conv_general_dilated(x, w[:, None, :], window_strides=(1,), padding=[(CONV_W - 1, 0)],
                                 dimension_numbers=('NWC', 'WIO', 'NWC'), feature_group_count=C)
    return y + b


def _rglru_branch(x_in, gate_in, conv_w, conv_b, w_a, b_a, w_x, b_x, lam):
    f32 = jnp.float32
    B, S, R = x_in.shape
    xc = _causal_depthwise_conv(x_in.astype(f32), conv_w.astype(f32), conv_b.astype(f32))
    xb = xc.reshape(B, S, LRU_BLOCKS, LRU_BD)
    r = jax.nn.sigmoid(jnp.einsum('bsnd,nde->bsne', xb, w_a).reshape(B, S, R) + b_a)
    i = jax.nn.sigmoid(jnp.einsum('bsnd,nde->bsne', xb, w_x).reshape(B, S, R) + b_x)
    log_a = -LRU_C * r * jax.nn.softplus(-lam)
    a = jnp.exp(log_a)
    u = jnp.sqrt(-jnp.expm1(2.0 * log_a)) * (i * xc)

    def combine(e1, e2):
        a1, u1 = e1
        a2, u2 = e2
        return a1 * a2, a2 * u1 + u2

    _, hseq = lax.associative_scan(combine, (a, u), axis=1)
    return hseq * jax.nn.gelu(gate_in.astype(f32))


def _rotary(t, positions):
    d = t.shape[-1]
    inv_freq = ROPE_BASE ** (-jnp.arange(0, d, 2, dtype=jnp.float32) / d)
    ang = positions.astype(jnp.float32)[..., None] * inv_freq
    cos = jnp.cos(ang)[:, :, None, :]
    sin = jnp.sin(ang)[:, :, None, :]
    t1, t2 = jnp.split(t, 2, axis=-1)
    return jnp.concatenate([t1 * cos - t2 * sin, t1 * sin + t2 * cos], axis=-1)


def _retention(q, k, v, g_pre, positions, norm_w):
    f32 = jnp.float32
    B, S, _ = q.shape
    H, dk, dv, L = RET_HEADS, RET_DK, RET_DV, CHUNK
    q = _rotary(q.astype(f32).reshape(B, S, H, dk), positions)
    k = _rotary(k.astype(f32).reshape(B, S, H, dk), positions) * dk ** -0.5
    v = v.astype(f32).reshape(B, S, H, dv)
    log_gamma = jnp.log1p(-jnp.exp2(-5.0 - jnp.arange(H, dtype=f32)))
    idx = jnp.arange(L, dtype=f32)
    rel = idx[:, None] - idx[None, :]
    intra = jnp.where(rel >= 0, jnp.exp(rel * log_gamma[:, None, None]), 0.0)
    q_decay = jnp.exp((idx + 1.0) * log_gamma[:, None])
    k_decay = jnp.exp((L - 1.0 - idx) * log_gamma[:, None])
    chunk_decay = jnp.exp(L * log_gamma)

    def step(state, inp):
        q_, k_, v_ = inp
        scores = jnp.einsum('bhtd,bhsd->bhts', q_, k_) * intra
        out = (jnp.einsum('bhts,bhse->bhte', scores, v_)
               + jnp.einsum('bhtd,bhde->bhte', q_ * q_decay[..., None], state))
        state = chunk_decay[:, None, None] * state + jnp.einsum('bhsd,bhse->bhde', k_ * k_decay[..., None], v_)
        return state, out

    init = jnp.zeros((B, H, dk, dv), f32)
    _, oc = lax.scan(step, init, (_to_chunks(q), _to_chunks(k), _to_chunks(v)))
    y = _head_norm(_from_chunks(oc), norm_w)
    return jax.nn.silu(g_pre.astype(f32)) * y


def _forgetting_attention(q, k, v, f_pre):
    f32 = jnp.float32
    B, S, _ = q.shape
    H, dh = FOX_HEADS, FOX_DH

    def heads(t):
        return t.astype(f32).reshape(B, S, H, dh).transpose(0, 2, 1, 3)

    q = heads(q) * dh ** -0.5
    k = heads(k)
    v = heads(v)
    cum_f = jnp.cumsum(jax.nn.log_sigmoid(f_pre.astype(f32)), axis=1).transpose(0, 2, 1)
    outs = []
    for blk in range(S // Q_BLOCK):
        lo, hi = blk * Q_BLOCK, (blk + 1) * Q_BLOCK
        logits = (jnp.einsum('bhqd,bhkd->bhqk', q[:, :, lo:hi], k[:, :, :hi])
                  + cum_f[:, :, lo:hi, None] - cum_f[:, :, None, :hi])
        mask = jnp.arange(lo, hi)[:, None] >= jnp.arange(hi)[None, :]
        p = jax.nn.softmax(jnp.where(mask, logits, -jnp.inf), axis=-1)
        outs.append(jnp.einsum('bhqk,bhkd->bhqd', p, v[:, :, :hi]))
    o = jnp.concatenate(outs, axis=2)
    return o.transpose(0, 2, 1, 3).reshape(B, S, H * dh)


def _hybrid_mixer(h, positions, w_in, b_in, m_norm, conv_w, conv_b, lru_wa, lru_ba, lru_wx, lru_bx,
                  lru_lam, r_norm, w_br, w_out, b_out):
    B, S, D = h.shape
    p = _split_combined(h @ w_in + b_in)
    y_m = _mlstm(p["m_q"], p["m_k"], p["m_v"], p["m_o"], p["m_i"], p["m_f"], m_norm)
    y_l = _rglru_branch(p["l_x"], p["l_g"], conv_w, conv_b, lru_wa, lru_ba, lru_wx, lru_bx, lru_lam)
    y_r = _retention(p["r_q"], p["r_k"], p["r_v"], p["r_g"], positions, r_norm)
    y_f = _forgetting_attention(p["f_q"], p["f_k"], p["f_v"], p["f_f"])
    branches = (y_m, y_l, y_r, y_f)
    gates = jax.nn.sigmoid(p["gate"].astype(jnp.float32))
    merged = gates[..., :D] * (branches[0].astype(h.dtype) @ w_br[0])
    for n in range(1, N_BRANCH):
        merged = merged + gates[..., n * D:(n + 1) * D] * (branches[n].astype(h.dtype) @ w_br[n])
    return merged.astype(h.dtype) @ w_out + b_out


def _sq_relu_mlp(h, w1, b1, w2, b2):
    return jnp.square(jax.nn.relu(h @ w1 + b1)) @ w2 + b2


def setup_inputs(seed: int = 0) -> dict:
    key = jax.random.key(seed)
    ks = jax.random.split(key, 32)
    f32 = jnp.float32

    def nrm(k, shape, s):
        return jax.random.normal(k, shape, f32) * s

    x = nrm(ks[0], (BATCH, SEQ, D_MODEL), 1.0)
    c = nrm(ks[1], (BATCH, D_MODEL), 1.0)
    offset = jax.random.randint(ks[2], (BATCH, 1), 0, 64, dtype=jnp.int32) * CHUNK
    positions = offset + jnp.arange(SEQ, dtype=jnp.int32)[None, :]
    w_ada = nrm(ks[3], (DEPTH, D_MODEL, 6 * D_MODEL), 0.25 * D_MODEL ** -0.5)
    b_ada = nrm(ks[4], (DEPTH, 6 * D_MODEL), 0.02)
    w_in = nrm(ks[5], (DEPTH, D_MODEL, N_IN), D_MODEL ** -0.5)
    seg_keys = jax.random.split(ks[6], len(SEGMENTS))
    parts = []
    for (name, width), sk in zip(SEGMENTS, seg_keys):
        base = nrm(sk, (DEPTH, width), 0.1 if name == "m_i" else 0.02)
        if name == "m_f":
            base = base + jnp.linspace(3.0, 6.0, width, dtype=f32)
        elif name == "f_f":
            base = base + jnp.linspace(1.0, 4.0, width, dtype=f32)
        parts.append(base)
    b_in = jnp.concatenate(parts, axis=-1)
    m_norm = 1.0 + nrm(ks[7], (DEPTH, MLSTM_W), 0.02)
    conv_w = nrm(ks[8], (DEPTH, CONV_W, LRU_W), CONV_W ** -0.5)
    conv_b = nrm(ks[9], (DEPTH, LRU_W), 0.02)
    lru_wa = nrm(ks[10], (DEPTH, LRU_BLOCKS, LRU_BD, LRU_BD), LRU_BD ** -0.5)
    lru_ba = nrm(ks[11], (DEPTH, LRU_W), 0.02)
    lru_wx = nrm(ks[12], (DEPTH, LRU_BLOCKS, LRU_BD, LRU_BD), LRU_BD ** -0.5)
    lru_bx = nrm(ks[13], (DEPTH, LRU_W), 0.02)
    u = jax.random.uniform(ks[14], (DEPTH, LRU_W), f32, 0.9, 0.999)
    pa = u ** (1.0 / LRU_C)
    lru_lam = jnp.log(pa) - jnp.log1p(-pa)
    r_norm = 1.0 + nrm(ks[15], (DEPTH, RET_W), 0.02)
    w_br = nrm(ks[16], (DEPTH, N_BRANCH, BRANCH_W, D_MODEL), BRANCH_W ** -0.5)
    w_out = nrm(ks[17], (DEPTH, D_MODEL, D_MODEL), DEEPNORM_BETA * D_MODEL ** -0.5)
    b_out = nrm(ks[18], (DEPTH, D_MODEL), 0.02)
    ln1_g = 1.0 + nrm(ks[19], (DEPTH, D_MODEL), 0.02)
    ln1_b = nrm(ks[20], (DEPTH, D_MODEL), 0.02)
    w_ff1 = nrm(ks[21], (DEPTH, D_MODEL, D_FF), D_MODEL ** -0.5)
    b_ff1 = nrm(ks[22], (DEPTH, D_FF), 0.02)
    w_ff2 = nrm(ks[23], (DEPTH, D_FF, D_MODEL), DEEPNORM_BETA * D_FF ** -0.5)
    b_ff2 = nrm(ks[24], (DEPTH, D_MODEL), 0.02)
    ln2_g = 1.0 + nrm(ks[25], (DEPTH, D_MODEL), 0.02)
    ln2_b = nrm(ks[26], (DEPTH, D_MODEL), 0.02)
    return {"x": x, "c": c, "positions": positions, "w_ada": w_ada, "b_ada": b_ada,
            "w_in": w_in, "b_in": b_in, "m_norm": m_norm, "conv_w": conv_w, "conv_b": conv_b,
            "lru_wa": lru_wa, "lru_ba": lru_ba, "lru_wx": lru_wx, "lru_bx": lru_bx, "lru_lam": lru_lam,
            "r_norm": r_norm, "w_br": w_br, "w_out": w_out, "b_out": b_out,
            "ln1_g": ln1_g, "ln1_b": ln1_b, "w_ff1": w_ff1, "b_ff1": b_ff1,
            "w_ff2": w_ff2, "b_ff2": b_ff2, "ln2_g": ln2_g, "ln2_b": ln2_b}


def reference(x, c, positions, w_ada, b_ada, w_in, b_in, m_norm, conv_w, conv_b,
              lru_wa, lru_ba, lru_wx, lru_bx, lru_lam, r_norm, w_br, w_out, b_out,
              ln1_g, ln1_b, w_ff1, b_ff1, w_ff2, b_ff2, ln2_g, ln2_b):
    cond = jax.nn.silu(c)
    for l in range(DEPTH):
        mod = cond @ w_ada[l] + b_ada[l]
        sh1, sc1, g1, sh2, sc2, g2 = (m[:, None, :] for m in jnp.split(mod, 6, axis=-1))
        h = _layernorm(x) * (1.0 + sc1) + sh1
        y = _hybrid_mixer(h, positions, w_in[l], b_in[l], m_norm[l], conv_w[l], conv_b[l],
                          lru_wa[l], lru_ba[l], lru_wx[l], lru_bx[l], lru_lam[l], r_norm[l],
                          w_br[l], w_out[l], b_out[l])
        x = _layernorm(DEEPNORM_ALPHA * x + (1.0 + g1) * y) * ln1_g[l] + ln1_b[l]
        h = _layernorm(x) * (1.0 + sc2) + sh2
        y = _sq_relu_mlp(h, w_ff1[l], b_ff1[l], w_ff2[l], b_ff2[l])
        x = _layernorm(DEEPNORM_ALPHA * x + (1.0 + g2) * y) * ln2_g[l] + ln2_b[l]
    return x
```

```python
import functools
import math

import jax
import jax.numpy as jnp
from jax import lax
from jax.experimental import pallas as pl
from jax.experimental.pallas import tpu as pltpu

F32 = jnp.float32
BF16 = jnp.bfloat16

LN_EPS = 1e-5
HEADS = 4
DH = 128
BRANCH_W = HEADS * DH
LRU_BLOCKS = 8
LRU_C = 8.0
CONV_W = 4
ROPE_BASE = 10000.0
N_BRANCH = 4

LANES = 128
SUBLANES = 8
CHUNK = 256
ROW_TILE = 512
VMEM_LIMIT_BYTES = 48 * 1024 * 1024

_SEGMENTS = (
    ("m_q", BRANCH_W), ("m_k", BRANCH_W), ("m_v", BRANCH_W), ("m_o", BRANCH_W),
    ("m_i", HEADS), ("m_f", HEADS),
    ("l_x", BRANCH_W), ("l_g", BRANCH_W),
    ("r_q", BRANCH_W), ("r_k", BRANCH_W), ("r_v", BRANCH_W), ("r_g", BRANCH_W),
    ("f_q", BRANCH_W), ("f_k", BRANCH_W), ("f_v", BRANCH_W), ("f_f", HEADS),
    ("gate", None),
)


def _segment_slices(d_model):
    out, acc = {}, 0
    for name, width in _SEGMENTS:
        width = N_BRANCH * d_model if width is None else width
        out[name] = (acc, acc + width)
        acc += width
    return out


def _dot(a, b):
    return jnp.dot(a, b, preferred_element_type=F32)


def _dot_f32(a, b):
    return jnp.dot(a, b, preferred_element_type=F32, precision=lax.Precision.HIGHEST)


def _layernorm(x):
    mu = jnp.mean(x, axis=-1, keepdims=True)
    xc = x - mu
    var = jnp.mean(xc * xc, axis=-1, keepdims=True)
    return xc * lax.rsqrt(var + LN_EPS)


def _log_sigmoid(x):
    return -(jnp.maximum(-x, 0.0) + jnp.log1p(jnp.exp(-jnp.abs(x))))


def _softplus(x):
    return jnp.maximum(x, 0.0) + jnp.log1p(jnp.exp(-jnp.abs(x)))


def _gelu_tanh(x):
    return x * (0.5 * (1.0 + jnp.tanh(math.sqrt(2.0 / math.pi) * (x + 0.044715 * (x * x * x)))))


def _causal_mask(n):
    row = lax.broadcasted_iota(jnp.int32, (n, n), 0)
    col = lax.broadcasted_iota(jnp.int32, (n, n), 1)
    return row >= col


def _prefix_sum_matrix(n):
    row = lax.broadcasted_iota(jnp.int32, (n, n), 0)
    col = lax.broadcasted_iota(jnp.int32, (n, n), 1)
    return jnp.where(row <= col, 1.0, 0.0).astype(F32)


def _ones_column_block(n):
    lane = lax.broadcasted_iota(jnp.int32, (n, LANES), 1)
    return jnp.where(lane == 0, 1.0, 0.0).astype(F32)


def _head_norm(y, gain):
    mu = jnp.mean(y, axis=-1, keepdims=True)
    yc = y - mu
    var = jnp.mean(yc * yc, axis=-1, keepdims=True)
    return yc * lax.rsqrt(var + LN_EPS) * gain


def _const_spec(shape):
    zeros = (0,) * len(shape)
    return pl.BlockSpec(shape, lambda *_: zeros, pipeline_mode=pl.Buffered(1))


def _params(semantics):
    return pltpu.CompilerParams(
        dimension_semantics=semantics, vmem_limit_bytes=VMEM_LIMIT_BYTES)


def _mod_kernel(c_ref, w_ref, b_ref, o_ref):
    c = c_ref[...]
    cond = c * jax.nn.sigmoid(c)
    o_ref[...] = _dot_f32(cond, w_ref[...]) + b_ref[...]


def _modulation(c, w_ada, b_ada):
    depth, d, six_d = w_ada.shape
    batch = c.shape[0]
    n_tiles = six_d // d
    return pl.pallas_call(
        _mod_kernel,
        grid=(depth, n_tiles),
        in_specs=[
            pl.BlockSpec((batch, d), lambda l, j: (0, 0)),
            pl.BlockSpec((None, d, d), lambda l, j: (l, 0, j)),
            pl.BlockSpec((None, 1, d), lambda l, j: (l, 0, j)),
        ],
        out_specs=pl.BlockSpec((None, batch, d), lambda l, j: (l, 0, j)),
        out_shape=jax.ShapeDtypeStruct((depth, batch, six_d), F32),
        compiler_params=_params(("parallel", "parallel")),
        name="adaln_modulation",
    )(c, w_ada, b_ada.reshape(depth, 1, six_d))


def _rope_kernel(pos_ref, invf_ref, cos_ref, sin_ref):
    ang = invf_ref[...] * pos_ref[...].astype(F32)
    c = jnp.cos(ang)
    s = jnp.sin(ang)
    cos_ref[...] = jnp.concatenate([c, c], axis=0).T
    sin_ref[...] = jnp.concatenate([-s, s], axis=0).T


def _rope_tables(positions):
    batch, seq = positions.shape
    inv_freq = ROPE_BASE ** (-jnp.arange(0, DH, 2, dtype=F32) / DH)
    out = jax.ShapeDtypeStruct((batch, seq, DH), F32)
    return pl.pallas_call(
        _rope_kernel,
        grid=(batch,),
        in_specs=[
            pl.BlockSpec((None, 1, seq), lambda b: (b, 0, 0)),
            pl.BlockSpec((DH // 2, 1), lambda b: (0, 0)),
        ],
        out_specs=[pl.BlockSpec((None, seq, DH), lambda b: (b, 0, 0))] * 2,
        out_shape=[out, out],
        compiler_params=_params(("parallel",)),
        name="rope_tables",
    )(positions.reshape(batch, 1, seq), inv_freq.reshape(DH // 2, 1))


def _ln_mod_kernel(x_ref, mod_ref, o_ref, *, shift_row, scale_row):
    xn = _layernorm(x_ref[...])
    scale = mod_ref[scale_row:scale_row + 1, :]
    shift = mod_ref[shift_row:shift_row + 1, :]
    o_ref[...] = (xn * (1.0 + scale) + shift).astype(BF16)


def _ln_modulate(x2d, mod_l, seq, shift_row, scale_row):
    tokens, d = x2d.shape
    tm = 2 * ROW_TILE
    tiles_per_seq = seq // tm
    return pl.pallas_call(
        functools.partial(_ln_mod_kernel, shift_row=shift_row, scale_row=scale_row),
        grid=(tokens // tm,),
        in_specs=[
            pl.BlockSpec((tm, d), lambda i: (i, 0)),
            pl.BlockSpec((None, 6, d), lambda i: (i // tiles_per_seq, 0, 0)),
        ],
        out_specs=pl.BlockSpec((tm, d), lambda i: (i, 0)),
        out_shape=jax.ShapeDtypeStruct((tokens, d), BF16),
        compiler_params=_params(("parallel",)),
        name="ln_modulate",
    )(x2d, mod_l)


def _mlstm_kernel(h_ref, w_ref, b_ref, gain_ref, o_ref, state_ref, mc_ref, bc_ref):
    seq = h_ref.shape[0]
    L = CHUNK
    qkvo = HEADS * DH
    state_ref[...] = jnp.zeros_like(state_ref)
    mc_ref[...] = jnp.zeros_like(mc_ref)
    bc_ref[...] = jnp.zeros_like(bc_ref)

    def step(i, carry):
        r0 = pl.multiple_of(i * L, L)
        z = _dot(h_ref[pl.ds(r0, L), :], w_ref[...]) + b_ref[...]
        causal = _causal_mask(L)
        gates = z[:, 4 * qkvo:4 * qkvo + LANES].T[0:2 * HEADS, :]
        log_f = _log_sigmoid(gates)
        cum_f = _dot_f32(log_f, _prefix_sum_matrix(L))
        ones_blk = _ones_column_block(L)
        for hd in range(HEADS):
            lo = hd * DH
            q = z[:, lo:lo + DH].astype(BF16)
            k_t = (z[:, qkvo + lo:qkvo + lo + DH] * DH ** -0.5).T
            v_aug = jnp.concatenate(
                [z[:, 2 * qkvo + lo:2 * qkvo + lo + DH], ones_blk], axis=1).astype(BF16)
            o_gate = jax.nn.sigmoid(z[:, 3 * qkvo + lo:3 * qkvo + lo + DH])
            mc = mc_ref[hd][:, 0:1]
            bc = bc_ref[hd][:, 0:1]
            lf_row = log_f[HEADS + hd:HEADS + hd + 1, :]
            b_row = cum_f[HEADS + hd:HEADS + hd + 1, :] + bc
            c_row = gates[hd:hd + 1, :] - b_row
            d0 = jnp.where(causal, c_row, -jnp.inf)
            m_t = jnp.maximum(jnp.max(d0, axis=1, keepdims=True), mc)
            p = jnp.exp(d0 - m_t)
            b_col = jnp.sum(jnp.where(causal, lf_row, 0.0), axis=1, keepdims=True) + bc
            s = _dot(q, k_t.astype(BF16))
            intra = _dot((s * p).astype(BF16), v_aug)
            st = state_ref[hd]
            inter = _dot(q, st.astype(BF16))
            tot = jnp.exp(mc - m_t) * inter + intra
            den = tot[:, DH:DH + 1]
            hh = tot[:, :DH] / jnp.maximum(jnp.abs(den), jnp.exp(-(b_col + m_t)))
            y = _head_norm(o_gate * hh, gain_ref[:, lo:lo + DH])
            o_ref[pl.ds(r0, L), lo:lo + DH] = y.astype(BF16)
            mc_new = m_t[L - 1:L, :]
            k_w = (k_t * jnp.exp(c_row - mc_new)).astype(BF16)
            state_ref[hd] = jnp.exp(mc - mc_new) * st + _dot(k_w, v_aug)
            mc_ref[hd] = jnp.broadcast_to(mc_new, (1, LANES))
            bc_ref[hd] = jnp.broadcast_to(b_row[:, L - 1:L], (1, LANES))
        return carry

    lax.fori_loop(0, seq // L, step, 0)


def _mlstm(h2d, w, b, gain, batch, seq):
    d = h2d.shape[1]
    n = w.shape[1]
    return pl.pallas_call(
        _mlstm_kernel,
        grid=(batch,),
        in_specs=[
            pl.BlockSpec((seq, d), lambda i: (i, 0)),
            _const_spec((d, n)),
            _const_spec((1, n)),
            _const_spec((1, BRANCH_W)),
        ],
        out_specs=pl.BlockSpec((seq, BRANCH_W), lambda i: (i, 0)),
        out_shape=jax.ShapeDtypeStruct((batch * seq, BRANCH_W), BF16),
        scratch_shapes=[
            pltpu.VMEM((HEADS, DH, 2 * DH), F32),
            pltpu.VMEM((HEADS, 1, LANES), F32),
            pltpu.VMEM((HEADS, 1, LANES), F32),
        ],
        compiler_params=_params(("arbitrary",)),
        name="mlstm",
    )(h2d, w, b, gain)


def _rglru_kernel(h_ref, w_ref, b_ref, cw_ref, cb_ref, wa_ref, ba_ref, wx_ref, bx_ref,
                  lam_ref, o_ref, xpad_ref, hbuf_ref, hcar_ref):
    seq = h_ref.shape[0]
    L = CHUNK
    R = BRANCH_W
    pad = SUBLANES
    xpad_ref[0:pad, :] = jnp.zeros((pad, R), F32)
    hcar_ref[...] = jnp.zeros_like(hcar_ref)

    def step(i, carry):
        r0 = pl.multiple_of(i * L, L)
        z = _dot(h_ref[pl.ds(r0, L), :], w_ref[...]) + b_ref[...]
        xpad_ref[pad:pad + L, :] = z[:, :R]
        xp = xpad_ref[...]
        xc = cb_ref[...]
        for k in range(CONV_W):
            off = pad - (CONV_W - 1) + k
            xc = xc + cw_ref[k:k + 1, :] * xp[off:off + L, :]
        xpad_ref[0:pad, :] = xp[L:L + pad, :]
        xcb = xc.astype(BF16)
        r = jax.nn.sigmoid(_dot(xcb, wa_ref[...]) + ba_ref[...])
        ig = jax.nn.sigmoid(_dot(xcb, wx_ref[...]) + bx_ref[...])
        log_a = (-LRU_C) * r * _softplus(-lam_ref[...])
        a = jnp.exp(log_a)
        th = jnp.tanh(log_a)
        u = jnp.sqrt(-2.0 * th / (1.0 - th)) * (ig * xc)
        rowmod = lax.broadcasted_iota(jnp.int32, (L, R), 0) & (SUBLANES - 1)
        for sft in (1, 2, 4):
            a_prev = pltpu.roll(a, sft, 0)
            u_prev = pltpu.roll(u, sft, 0)
            ok = rowmod >= sft
            u = jnp.where(ok, a * u_prev + u, u)
            a = jnp.where(ok, a * a_prev, a)
        hprev = hcar_ref[...]
        for g in range(L // SUBLANES):
            rows = slice(g * SUBLANES, (g + 1) * SUBLANES)
            hg = a[rows, :] * hprev + u[rows, :]
            hbuf_ref[rows, :] = hg
            hprev = hg[SUBLANES - 1:SUBLANES, :]
        hcar_ref[...] = hprev
        o_ref[pl.ds(r0, L), :] = (hbuf_ref[...] * _gelu_tanh(z[:, R:2 * R])).astype(BF16)
        return carry

    lax.fori_loop(0, seq // L, step, 0)


def _rglru(h2d, w, b, conv_w, conv_b, wa, ba, wx, bx, lam, batch, seq):
    d = h2d.shape[1]
    R = BRANCH_W
    return pl.pallas_call(
        _rglru_kernel,
        grid=(batch,),
        in_specs=[
            pl.BlockSpec((seq, d), lambda i: (i, 0)),
            _const_spec((d, 2 * R)),
            _const_spec((1, 2 * R)),
            _const_spec((CONV_W, R)),
            _const_spec((1, R)),
            _const_spec((R, R)),
            _const_spec((1, R)),
            _const_spec((R, R)),
            _const_spec((1, R)),
            _const_spec((1, R)),
        ],
        out_specs=pl.BlockSpec((seq, R), lambda i: (i, 0)),
        out_shape=jax.ShapeDtypeStruct((batch * seq, R), BF16),
        scratch_shapes=[
            pltpu.VMEM((CHUNK + SUBLANES, R), F32),
            pltpu.VMEM((CHUNK, R), F32),
            pltpu.VMEM((1, R), F32),
        ],
        compiler_params=_params(("arbitrary",)),
        name="rglru",
    )(h2d, w, b, conv_w, conv_b, wa, ba, wx, bx, lam)


def _log_gamma(hd):
    return math.log1p(-(2.0 ** (-5.0 - hd)))


def _retention_kernel(h_ref, w_ref, b_ref, cos_ref, sin_ref, gain_ref, o_ref, state_ref):
    seq = h_ref.shape[0]
    L = CHUNK
    W = HEADS * DH
    state_ref[...] = jnp.zeros_like(state_ref)

    def step(i, carry):
        r0 = pl.multiple_of(i * L, L)
        z = _dot(h_ref[pl.ds(r0, L), :], w_ref[...]) + b_ref[...]
        cosf = cos_ref[pl.ds(r0, L), :]
        sinf = sin_ref[pl.ds(r0, L), :]
        row = lax.broadcasted_iota(jnp.int32, (L, L), 0)
        col = lax.broadcasted_iota(jnp.int32, (L, L), 1)
        rel = (row - col).astype(F32)
        idx_col = lax.broadcasted_iota(jnp.int32, (L, 1), 0).astype(F32)
        idx_row = lax.broadcasted_iota(jnp.int32, (1, L), 1).astype(F32)
        for hd in range(HEADS):
            lo = hd * DH
            lg = _log_gamma(hd)
            qf = z[:, lo:lo + DH]
            kf = z[:, W + lo:W + lo + DH]
            q = (qf * cosf + pltpu.roll(qf, DH // 2, 1) * sinf).astype(BF16)
            k_t = ((kf * cosf + pltpu.roll(kf, DH // 2, 1) * sinf) * DH ** -0.5).T
            v = z[:, 2 * W + lo:2 * W + lo + DH].astype(BF16)
            decay = jnp.where(rel >= 0.0, jnp.exp(rel * lg), 0.0)
            s = _dot(q, k_t.astype(BF16)) * decay
            st = state_ref[hd]
            out = _dot(s.astype(BF16), v) + jnp.exp((idx_col + 1.0) * lg) * _dot(q, st.astype(BF16))
            k_w = (k_t * jnp.exp((L - 1.0 - idx_row) * lg)).astype(BF16)
            state_ref[hd] = math.exp(L * lg) * st + _dot(k_w, v)
            g = z[:, 3 * W + lo:3 * W + lo + DH]
            y = (g * jax.nn.sigmoid(g)) * _head_norm(out, gain_ref[:, lo:lo + DH])
            o_ref[pl.ds(r0, L), lo:lo + DH] = y.astype(BF16)
        return carry

    lax.fori_loop(0, seq // L, step, 0)


def _retention(h2d, w, b, cos_t, sin_t, gain, batch, seq):
    d = h2d.shape[1]
    n = w.shape[1]
    return pl.pallas_call(
        _retention_kernel,
        grid=(batch,),
        in_specs=[
            pl.BlockSpec((seq, d), lambda i: (i, 0)),
            _const_spec((d, n)),
            _const_spec((1, n)),
            pl.BlockSpec((None, seq, DH), lambda i: (i, 0, 0)),
            pl.BlockSpec((None, seq, DH), lambda i: (i, 0, 0)),
            _const_spec((1, BRANCH_W)),
        ],
        out_specs=pl.BlockSpec((seq, BRANCH_W), lambda i: (i, 0)),
        out_shape=jax.ShapeDtypeStruct((batch * seq, BRANCH_W), BF16),
        scratch_shapes=[pltpu.VMEM((HEADS, DH, DH), F32)],
        compiler_params=_params(("arbitrary",)),
        name="retention",
    )(h2d, w, b, cos_t, sin_t, gain)


def _fox_kernel(h_ref, w_ref, b_ref, o_ref, kt_ref, v_ref, f_ref, fc_ref):
    seq = h_ref.shape[0]
    L = CHUNK
    W = HEADS * DH
    fc_ref[...] = jnp.zeros_like(fc_ref)

    def step(i, carry):
        r0 = pl.multiple_of(i * L, L)
        z = _dot(h_ref[pl.ds(r0, L), :], w_ref[...]) + b_ref[...]
        causal = _causal_mask(L)
        f_pre = z[:, 3 * W:3 * W + LANES].T[0:SUBLANES, :]
        cum_f = _dot_f32(_log_sigmoid(f_pre), _prefix_sum_matrix(L)) + fc_ref[:, 0:1]
        f_ref[i] = cum_f
        fc_ref[...] = jnp.broadcast_to(cum_f[:, L - 1:L], (SUBLANES, LANES))
        ones_blk = _ones_column_block(L)
        for hd in range(HEADS):
            lo = hd * DH
            kt_ref[hd, i] = z[:, W + lo:W + lo + DH].T.astype(BF16)
            v_ref[hd, i] = jnp.concatenate(
                [z[:, 2 * W + lo:2 * W + lo + DH], ones_blk], axis=1).astype(BF16)
        for hd in range(HEADS):
            lo = hd * DH
            q = (z[:, lo:lo + DH] * DH ** -0.5).astype(BF16)
            s = _dot(q, kt_ref[hd, i]) - cum_f[hd:hd + 1, :]
            s = jnp.where(causal, s, -jnp.inf)
            m = jnp.max(s, axis=1, keepdims=True)
            acc = _dot(jnp.exp(s - m).astype(BF16), v_ref[hd, i])

            def kv_step(j, mc, q=q, hd=hd):
                m_old, acc_old = mc
                sj = _dot(q, kt_ref[hd, j]) - f_ref[j][hd:hd + 1, :]
                m_new = jnp.maximum(m_old, jnp.max(sj, axis=1, keepdims=True))
                pj = jnp.exp(sj - m_new).astype(BF16)
                return m_new, jnp.exp(m_old - m_new) * acc_old + _dot(pj, v_ref[hd, j])

            m, acc = lax.fori_loop(0, i, kv_step, (m, acc))
            o_ref[pl.ds(r0, L), lo:lo + DH] = (acc[:, :DH] / acc[:, DH:DH + 1]).astype(BF16)
        return carry

    lax.fori_loop(0, seq // L, step, 0)


def _fox(h2d, w, b, batch, seq):
    d = h2d.shape[1]
    n = w.shape[1]
    nc = seq // CHUNK
    return pl.pallas_call(
        _fox_kernel,
        grid=(batch,),
        in_specs=[
            pl.BlockSpec((seq, d), lambda i: (i, 0)),
            _const_spec((d, n)),
            _const_spec((1, n)),
        ],
        out_specs=pl.BlockSpec((seq, BRANCH_W), lambda i: (i, 0)),
        out_shape=jax.ShapeDtypeStruct((batch * seq, BRANCH_W), BF16),
        scratch_shapes=[
            pltpu.VMEM((HEADS, nc, DH, CHUNK), BF16),
            pltpu.VMEM((HEADS, nc, CHUNK, 2 * DH), BF16),
            pltpu.VMEM((nc, SUBLANES, CHUNK), F32),
            pltpu.VMEM((SUBLANES, LANES), F32),
        ],
        compiler_params=_params(("arbitrary",)),
        name="forgetting_attention",
    )(h2d, w, b)


def _merge_kernel(h_ref, ym_ref, yl_ref, yr_ref, yf_ref, x_ref, mod_ref, wg_ref, bg_ref,
                  wbr_ref, wo_ref, bo_ref, lng_ref, lnb_ref, o_ref, *, alpha, gate_row):
    d = x_ref.shape[1]
    h = h_ref[...]
    merged = None
    for n, y_ref in enumerate((ym_ref, yl_ref, yr_ref, yf_ref)):
        gate = jax.nn.sigmoid(_dot(h, wg_ref[:, n * d:(n + 1) * d]) + bg_ref[:, n * d:(n + 1) * d])
        term = gate * _dot(y_ref[...], wbr_ref[n])
        merged = term if merged is None else merged + term
    y = _dot(merged.astype(BF16), wo_ref[...]) + bo_ref[...]
    res = alpha * x_ref[...] + (1.0 + mod_ref[gate_row:gate_row + 1, :]) * y
    o_ref[...] = _layernorm(res) * lng_ref[...] + lnb_ref[...]


def _merge(h2d, ys, x2d, mod_l, wg, bg, wbr, wo, bo, lng, lnb, seq, alpha):
    tokens, d = x2d.shape
    tm = ROW_TILE
    tiles_per_seq = seq // tm
    row = lambda i: (i, 0)
    return pl.pallas_call(
        functools.partial(_merge_kernel, alpha=alpha, gate_row=2),
        grid=(tokens // tm,),
        in_specs=[
            pl.BlockSpec((tm, d), row),
            pl.BlockSpec((tm, BRANCH_W), row),
            pl.BlockSpec((tm, BRANCH_W), row),
            pl.BlockSpec((tm, BRANCH_W), row),
            pl.BlockSpec((tm, BRANCH_W), row),
            pl.BlockSpec((tm, d), row),
            pl.BlockSpec((None, 6, d), lambda i: (i // tiles_per_seq, 0, 0)),
            _const_spec((d, N_BRANCH * d)),
            _const_spec((1, N_BRANCH * d)),
            _const_spec((N_BRANCH, BRANCH_W, d)),
            _const_spec((d, d)),
            _const_spec((1, d)),
            _const_spec((1, d)),
            _const_spec((1, d)),
        ],
        out_specs=pl.BlockSpec((tm, d), row),
        out_shape=jax.ShapeDtypeStruct((tokens, d), F32),
        compiler_params=_params(("parallel",)),
        name="merge_out_proj",
    )(h2d, *ys, x2d, mod_l, wg, bg, wbr, wo, bo, lng, lnb)


def _mlp_kernel(h_ref, x_ref, mod_ref, w1_ref, b1_ref, w2_ref, b2_ref, lng_ref, lnb_ref,
                o_ref, *, alpha, gate_row, ff_tile):
    d_ff = w1_ref.shape[1]
    h = h_ref[...]
    acc = None
    for c in range(d_ff // ff_tile):
        cols = slice(c * ff_tile, (c + 1) * ff_tile)
        a = jnp.maximum(_dot(h, w1_ref[:, cols]) + b1_ref[:, cols], 0.0)
        part = _dot((a * a).astype(BF16), w2_ref[cols, :])
        acc = part if acc is None else acc + part
    y = acc + b2_ref[...]
    res = alpha * x_ref[...] + (1.0 + mod_ref[gate_row:gate_row + 1, :]) * y
    o_ref[...] = _layernorm(res) * lng_ref[...] + lnb_ref[...]


def _mlp(h2d, x2d, mod_l, w1, b1, w2, b2, lng, lnb, seq, alpha):
    tokens, d = x2d.shape
    d_ff = w1.shape[1]
    tm = ROW_TILE
    tiles_per_seq = seq // tm
    row = lambda i: (i, 0)
    return pl.pallas_call(
        functools.partial(_mlp_kernel, alpha=alpha, gate_row=5, ff_tile=d),
        grid=(tokens // tm,),
        in_specs=[
            pl.BlockSpec((tm, d), row),
            pl.BlockSpec((tm, d), row),
            pl.BlockSpec((None, 6, d), lambda i: (i // tiles_per_seq, 0, 0)),
            _const_spec((d, d_ff)),
            _const_spec((1, d_ff)),
            _const_spec((d_ff, d)),
            _const_spec((1, d)),
            _const_spec((1, d)),
            _const_spec((1, d)),
        ],
        out_specs=pl.BlockSpec((tm, d), row),
        out_shape=jax.ShapeDtypeStruct((tokens, d), F32),
        compiler_params=_params(("parallel",)),
        name="sq_relu_mlp",
    )(h2d, x2d, mod_l, w1, b1, w2, b2, lng, lnb)


def _pack_proj(w_in_l, b_in_l, seg, names, gate_names=()):
    ws = [w_in_l[:, seg[n][0]:seg[n][1]] for n in names]
    bs = [b_in_l[seg[n][0]:seg[n][1]] for n in names]
    if gate_names:
        gw = jnp.concatenate([w_in_l[:, seg[n][0]:seg[n][1]] for n in gate_names], axis=1)
        gb = jnp.concatenate([b_in_l[seg[n][0]:seg[n][1]] for n in gate_names])
        padn = LANES - gw.shape[1]
        ws.append(jnp.pad(gw, ((0, 0), (0, padn))))
        bs.append(jnp.pad(gb, (0, padn)))
    return jnp.concatenate(ws, axis=1).astype(BF16), jnp.concatenate(bs)[None, :]


def _block_diag(w):
    nb, bd, _ = w.shape
    eye = jnp.eye(nb, dtype=w.dtype)
    return (eye[:, None, :, None] * w[:, :, None, :]).reshape(nb * bd, nb * bd)


def kernel(x, c, positions, w_ada, b_ada, w_in, b_in, m_norm, conv_w, conv_b, lru_wa, lru_ba,
           lru_wx, lru_bx, lru_lam, r_norm, w_br, w_out, b_out, ln1_g, ln1_b, w_ff1, b_ff1,
           w_ff2, b_ff2, ln2_g, ln2_b):
    batch, seq, d = x.shape
    depth = w_ada.shape[0]
    assert seq % CHUNK == 0 and seq % (2 * ROW_TILE) == 0 and d % LANES == 0
    alpha = (2 * depth) ** 0.25
    seg = _segment_slices(d)

    mod = _modulation(c, w_ada, b_ada).reshape(depth, batch, 6, d)
    cos_t, sin_t = _rope_tables(positions)
    x2d = x.reshape(batch * seq, d)
    for l in range(depth):
        wl, bl = w_in[l], b_in[l]
        w_m, b_m = _pack_proj(wl, bl, seg, ("m_q", "m_k", "m_v", "m_o"), ("m_i", "m_f"))
        w_l, b_l = _pack_proj(wl, bl, seg, ("l_x", "l_g"))
        w_r, b_r = _pack_proj(wl, bl, seg, ("r_q", "r_k", "r_v", "r_g"))
        w_f, b_f = _pack_proj(wl, bl, seg, ("f_q", "f_k", "f_v"), ("f_f",))
        w_g, b_g = _pack_proj(wl, bl, seg, ("gate",))
        row = lambda v: v[None, :]

        h1 = _ln_modulate(x2d, mod[l], seq, shift_row=0, scale_row=1)
        y_m = _mlstm(h1, w_m, b_m, row(m_norm[l]), batch, seq)
        y_l = _rglru(h1, w_l, b_l, conv_w[l], row(conv_b[l]),
                     _block_diag(lru_wa[l]).astype(BF16), row(lru_ba[l]),
                     _block_diag(lru_wx[l]).astype(BF16), row(lru_bx[l]),
                     row(lru_lam[l]), batch, seq)
        y_r = _retention(h1, w_r, b_r, cos_t, sin_t, row(r_norm[l]), batch, seq)
        y_f = _fox(h1, w_f, b_f, batch, seq)
        x2d = _merge(h1, (y_m, y_l, y_r, y_f), x2d, mod[l], w_g, b_g, w_br[l].astype(BF16),
                     w_out[l].astype(BF16), row(b_out[l]), row(ln1_g[l]), row(ln1_b[l]),
                     seq, alpha)
        h2 = _ln_modulate(x2d, mod[l], seq, shift_row=3, scale_row=4)
        x2d = _mlp(h2, x2d, mod[l], w_ff1[l].astype(BF16), row(b_ff1[l]),
                   w_ff2[l].astype(BF16), row(b_ff2[l]), row(ln2_g[l]), row(ln2_b[l]),
                   seq, alpha)
    return x2d.reshape(batch, seq, d)
```

```python
import functools
import math

import jax
import jax.numpy as jnp
from jax import lax
from jax.experimental import pallas as pl
from jax.experimental.pallas import tpu as pltpu

F32 = jnp.float32
BF16 = jnp.bfloat16

LN_EPS = 1e-5
HEADS = 4
DH = 128
BRANCH_W = HEADS * DH
LRU_BLOCKS = 8
LRU_C = 8.0
CONV_W = 4
ROPE_BASE = 10000.0
N_BRANCH = 4

LANES = 128
SUBLANES = 8
CHUNK = 256
ROW_TILE = 512
VMEM_LIMIT_BYTES = 48 * 1024 * 1024

_SEGMENTS = (
    ("m_q", BRANCH_W), ("m_k", BRANCH_W), ("m_v", BRANCH_W), ("m_o", BRANCH_W),
    ("m_i", HEADS), ("m_f", HEADS),
    ("l_x", BRANCH_W), ("l_g", BRANCH_W),
    ("r_q", BRANCH_W), ("r_k", BRANCH_W), ("r_v", BRANCH_W), ("r_g", BRANCH_W),
    ("f_q", BRANCH_W), ("f_k", BRANCH_W), ("f_v", BRANCH_W), ("f_f", HEADS),
    ("gate", None),
)


def _segment_slices(d_model):
    out, acc = {}, 0
    for name, width in _SEGMENTS:
        width = N_BRANCH * d_model if width is None else width
        out[name] = (acc, acc + width)
        acc += width
    return out


def _dot(a, b):
    return jnp.dot(a, b, preferred_element_type=F32)


def _dot_f32(a, b):
    return jnp.dot(a, b, preferred_element_type=F32, precision=lax.Precision.HIGHEST)


def _layernorm(x):
    mu = jnp.mean(x, axis=-1, keepdims=True)
    xc = x - mu
    var = jnp.mean(xc * xc, axis=-1, keepdims=True)
    return xc * lax.rsqrt(var + LN_EPS)


def _log_sigmoid(x):
    return -(jnp.maximum(-x, 0.0) + jnp.log1p(jnp.exp(-jnp.abs(x))))


def _softplus(x):
    return jnp.maximum(x, 0.0) + jnp.log1p(jnp.exp(-jnp.abs(x)))


def _gelu_tanh(x):
    return x * (0.5 * (1.0 + jnp.tanh(math.sqrt(2.0 / math.pi) * (x + 0.044715 * (x * x * x)))))


def _causal_mask(n):
    row = lax.broadcasted_iota(jnp.int32, (n, n), 0)
    col = lax.broadcasted_iota(jnp.int32, (n, n), 1)
    return row >= col


def _prefix_sum_matrix(n):
    row = lax.broadcasted_iota(jnp.int32, (n, n), 0)
    col = lax.broadcasted_iota(jnp.int32, (n, n), 1)
    return jnp.where(row <= col, 1.0, 0.0).astype(F32)


def _ones_column_block(n):
    lane = lax.broadcasted_iota(jnp.int32, (n, LANES), 1)
    return jnp.where(lane == 0, 1.0, 0.0).astype(F32)


def _head_norm(y, gain):
    mu = jnp.mean(y, axis=-1, keepdims=True)
    yc = y - mu
    var = jnp.mean(yc * yc, axis=-1, keepdims=True)
    return yc * lax.rsqrt(var + LN_EPS) * gain


def _const_spec(shape):
    zeros = (0,) * len(shape)
    return pl.BlockSpec(shape, lambda *_: zeros, pipeline_mode=pl.Buffered(1))


def _layer_spec(shape, layer):
    idx = (layer,) + (0,) * len(shape)
    return pl.BlockSpec((None,) + tuple(shape), lambda *_: idx, pipeline_mode=pl.Buffered(1))


def _params(semantics):
    return pltpu.CompilerParams(
        dimension_semantics=semantics, vmem_limit_bytes=VMEM_LIMIT_BYTES)


def _mod_kernel(c_ref, w_ref, b_ref, o_ref):
    c = c_ref[...]
    cond = c * jax.nn.sigmoid(c)
    o_ref[...] = _dot_f32(cond, w_ref[...]) + b_ref[...]


def _modulation(c, w_ada, b_ada):
    depth, d, six_d = w_ada.shape
    batch = c.shape[0]
    n_tiles = six_d // d
    return pl.pallas_call(
        _mod_kernel,
        grid=(depth, n_tiles),
        in_specs=[
            pl.BlockSpec((batch, d), lambda l, j: (0, 0)),
            pl.BlockSpec((None, d, d), lambda l, j: (l, 0, j)),
            pl.BlockSpec((None, 1, d), lambda l, j: (l, 0, j)),
        ],
        out_specs=pl.BlockSpec((None, batch, d), lambda l, j: (l, 0, j)),
        out_shape=jax.ShapeDtypeStruct((depth, batch, six_d), F32),
        compiler_params=_params(("parallel", "parallel")),
        name="adaln_modulation",
    )(c, w_ada, b_ada.reshape(depth, 1, six_d))


def _rope_kernel(pos_ref, invf_ref, cos_ref, sin_ref):
    ang = invf_ref[...] * pos_ref[...].astype(F32)
    c = jnp.cos(ang)
    s = jnp.sin(ang)
    cos_ref[...] = jnp.concatenate([c, c], axis=0).T
    sin_ref[...] = jnp.concatenate([-s, s], axis=0).T


def _rope_tables(positions):
    batch, seq = positions.shape
    inv_freq = ROPE_BASE ** (-jnp.arange(0, DH, 2, dtype=F32) / DH)
    out = jax.ShapeDtypeStruct((batch, seq, DH), F32)
    return pl.pallas_call(
        _rope_kernel,
        grid=(batch,),
        in_specs=[
            pl.BlockSpec((None, 1, seq), lambda b: (b, 0, 0)),
            pl.BlockSpec((DH // 2, 1), lambda b: (0, 0)),
        ],
        out_specs=[pl.BlockSpec((None, seq, DH), lambda b: (b, 0, 0))] * 2,
        out_shape=[out, out],
        compiler_params=_params(("parallel",)),
        name="rope_tables",
    )(positions.reshape(batch, 1, seq), inv_freq.reshape(DH // 2, 1))


def _ln_modulated(x, mod_ref, shift_row, scale_row):
    scale = mod_ref[scale_row:scale_row + 1, :]
    shift = mod_ref[shift_row:shift_row + 1, :]
    return (_layernorm(x) * (1.0 + scale) + shift).astype(BF16)


def _ln_mod_kernel(x_ref, mod_ref, o_ref, *, shift_row, scale_row):
    o_ref[...] = _ln_modulated(x_ref[...], mod_ref, shift_row, scale_row)


def _ln_modulate(x2d, mod_l, seq, shift_row, scale_row):
    tokens, d = x2d.shape
    tm = 2 * ROW_TILE
    tiles_per_seq = seq // tm
    return pl.pallas_call(
        functools.partial(_ln_mod_kernel, shift_row=shift_row, scale_row=scale_row),
        grid=(tokens // tm,),
        in_specs=[
            pl.BlockSpec((tm, d), lambda i: (i, 0)),
            pl.BlockSpec((None, 6, d), lambda i: (i // tiles_per_seq, 0, 0)),
        ],
        out_specs=pl.BlockSpec((tm, d), lambda i: (i, 0)),
        out_shape=jax.ShapeDtypeStruct((tokens, d), BF16),
        compiler_params=_params(("parallel",)),
        name="ln_modulate",
    )(x2d, mod_l)


def _mlstm_kernel(h_ref, w_ref, b_ref, gain_ref, o_ref, state_ref, mc_ref, bc_ref):
    seq = h_ref.shape[0]
    L = CHUNK
    qkvo = HEADS * DH
    state_ref[...] = jnp.zeros_like(state_ref)
    mc_ref[...] = jnp.zeros_like(mc_ref)
    bc_ref[...] = jnp.zeros_like(bc_ref)

    def step(i, carry):
        r0 = pl.multiple_of(i * L, L)
        z = _dot(h_ref[pl.ds(r0, L), :], w_ref[...]) + b_ref[...]
        causal = _causal_mask(L)
        gates = z[:, 4 * qkvo:4 * qkvo + LANES].T[0:2 * HEADS, :]
        log_f = _log_sigmoid(gates)
        cum_f = _dot_f32(log_f, _prefix_sum_matrix(L))
        ones_blk = _ones_column_block(L)
        for hd in range(HEADS):
            lo = hd * DH
            q = z[:, lo:lo + DH].astype(BF16)
            k_t = (z[:, qkvo + lo:qkvo + lo + DH] * DH ** -0.5).T
            v_aug = jnp.concatenate(
                [z[:, 2 * qkvo + lo:2 * qkvo + lo + DH], ones_blk], axis=1).astype(BF16)
            o_gate = jax.nn.sigmoid(z[:, 3 * qkvo + lo:3 * qkvo + lo + DH])
            mc = mc_ref[hd][:, 0:1]
            bc = bc_ref[hd][:, 0:1]
            lf_row = log_f[HEADS + hd:HEADS + hd + 1, :]
            b_row = cum_f[HEADS + hd:HEADS + hd + 1, :] + bc
            c_row = gates[hd:hd + 1, :] - b_row
            d0 = jnp.where(causal, c_row, -jnp.inf)
            m_t = jnp.maximum(jnp.max(d0, axis=1, keepdims=True), mc)
            p = jnp.exp(d0 - m_t)
            b_col = jnp.sum(jnp.where(causal, lf_row, 0.0), axis=1, keepdims=True) + bc
            s = _dot(q, k_t.astype(BF16))
            intra = _dot((s * p).astype(BF16), v_aug)
            st = state_ref[hd]
            inter = _dot(q, st.astype(BF16))
            tot = jnp.exp(mc - m_t) * inter + intra
            den = tot[:, DH:DH + 1]
            hh = tot[:, :DH] / jnp.maximum(jnp.abs(den), jnp.exp(-(b_col + m_t)))
            y = _head_norm(o_gate * hh, gain_ref[:, lo:lo + DH])
            o_ref[pl.ds(r0, L), lo:lo + DH] = y.astype(BF16)
            mc_new = m_t[L - 1:L, :]
            k_w = (k_t * jnp.exp(c_row - mc_new)).astype(BF16)
            state_ref[hd] = jnp.exp(mc - mc_new) * st + _dot(k_w, v_aug)
            mc_ref[hd] = jnp.broadcast_to(mc_new, (1, LANES))
            bc_ref[hd] = jnp.broadcast_to(b_row[:, L - 1:L], (1, LANES))
        return carry

    lax.fori_loop(0, seq // L, step, 0)


def _mlstm(h2d, w, layer, b, gain, batch, seq):
    d = h2d.shape[1]
    n = w.shape[2]
    return pl.pallas_call(
        _mlstm_kernel,
        grid=(batch,),
        in_specs=[
            pl.BlockSpec((seq, d), lambda i: (i, 0)),
            _layer_spec((d, n), layer),
            _const_spec((1, n)),
            _const_spec((1, BRANCH_W)),
        ],
        out_specs=pl.BlockSpec((seq, BRANCH_W), lambda i: (i, 0)),
        out_shape=jax.ShapeDtypeStruct((batch * seq, BRANCH_W), BF16),
        scratch_shapes=[
            pltpu.VMEM((HEADS, DH, 2 * DH), F32),
            pltpu.VMEM((HEADS, 1, LANES), F32),
            pltpu.VMEM((HEADS, 1, LANES), F32),
        ],
        compiler_params=_params(("arbitrary",)),
        name="mlstm",
    )(h2d, w, b, gain)


def _rglru_kernel(h_ref, w_ref, b_ref, cw_ref, cb_ref, wa_ref, ba_ref, wx_ref, bx_ref,
                  lam_ref, o_ref, xpad_ref, hbuf_ref, hcar_ref):
    seq = h_ref.shape[0]
    L = CHUNK
    R = BRANCH_W
    pad = SUBLANES
    xpad_ref[0:pad, :] = jnp.zeros((pad, R), F32)
    hcar_ref[...] = jnp.zeros_like(hcar_ref)

    def step(i, carry):
        r0 = pl.multiple_of(i * L, L)
        z = _dot(h_ref[pl.ds(r0, L), :], w_ref[...]) + b_ref[...]
        xpad_ref[pad:pad + L, :] = z[:, :R]
        xp = xpad_ref[...]
        xc = cb_ref[...]
        for k in range(CONV_W):
            off = pad - (CONV_W - 1) + k
            xc = xc + cw_ref[k:k + 1, :] * xp[off:off + L, :]
        xpad_ref[0:pad, :] = xp[L:L + pad, :]
        xcb = xc.astype(BF16)
        r = jax.nn.sigmoid(_dot(xcb, wa_ref[...]) + ba_ref[...])
        ig = jax.nn.sigmoid(_dot(xcb, wx_ref[...]) + bx_ref[...])
        log_a = (-LRU_C) * r * _softplus(-lam_ref[...])
        a = jnp.exp(log_a)
        th = jnp.tanh(log_a)
        u = jnp.sqrt(-2.0 * th / (1.0 - th)) * (ig * xc)
        rowmod = lax.broadcasted_iota(jnp.int32, (L, R), 0) & (SUBLANES - 1)
        for sft in (1, 2, 4):
            a_prev = pltpu.roll(a, sft, 0)
            u_prev = pltpu.roll(u, sft, 0)
            ok = rowmod >= sft
            u = jnp.where(ok, a * u_prev + u, u)
            a = jnp.where(ok, a * a_prev, a)
        hprev = hcar_ref[...]
        for g in range(L // SUBLANES):
            rows = slice(g * SUBLANES, (g + 1) * SUBLANES)
            hg = a[rows, :] * hprev + u[rows, :]
            hbuf_ref[rows, :] = hg
            hprev = hg[SUBLANES - 1:SUBLANES, :]
        hcar_ref[...] = hprev
        o_ref[pl.ds(r0, L), :] = (hbuf_ref[...] * _gelu_tanh(z[:, R:2 * R])).astype(BF16)
        return carry

    lax.fori_loop(0, seq // L, step, 0)


def _rglru(h2d, w, layer, b, conv_w, conv_b, wa, ba, wx, bx, lam, batch, seq):
    d = h2d.shape[1]
    R = BRANCH_W
    return pl.pallas_call(
        _rglru_kernel,
        grid=(batch,),
        in_specs=[
            pl.BlockSpec((seq, d), lambda i: (i, 0)),
            _layer_spec((d, 2 * R), layer),
            _const_spec((1, 2 * R)),
            _const_spec((CONV_W, R)),
            _const_spec((1, R)),
            _const_spec((R, R)),
            _const_spec((1, R)),
            _const_spec((R, R)),
            _const_spec((1, R)),
            _const_spec((1, R)),
        ],
        out_specs=pl.BlockSpec((seq, R), lambda i: (i, 0)),
        out_shape=jax.ShapeDtypeStruct((batch * seq, R), BF16),
        scratch_shapes=[
            pltpu.VMEM((CHUNK + SUBLANES, R), F32),
            pltpu.VMEM((CHUNK, R), F32),
            pltpu.VMEM((1, R), F32),
        ],
        compiler_params=_params(("arbitrary",)),
        name="rglru",
    )(h2d, w, b, conv_w, conv_b, wa, ba, wx, bx, lam)


def _log_gamma(hd):
    return math.log1p(-(2.0 ** (-5.0 - hd)))


def _retention_kernel(h_ref, w_ref, b_ref, cos_ref, sin_ref, gain_ref, o_ref, state_ref):
    seq = h_ref.shape[0]
    L = CHUNK
    W = HEADS * DH
    state_ref[...] = jnp.zeros_like(state_ref)

    def step(i, carry):
        r0 = pl.multiple_of(i * L, L)
        z = _dot(h_ref[pl.ds(r0, L), :], w_ref[...]) + b_ref[...]
        cosf = cos_ref[pl.ds(r0, L), :]
        sinf = sin_ref[pl.ds(r0, L), :]
        row = lax.broadcasted_iota(jnp.int32, (L, L), 0)
        col = lax.broadcasted_iota(jnp.int32, (L, L), 1)
        rel = (row - col).astype(F32)
        idx_col = lax.broadcasted_iota(jnp.int32, (L, 1), 0).astype(F32)
        idx_row = lax.broadcasted_iota(jnp.int32, (1, L), 1).astype(F32)
        for hd in range(HEADS):
            lo = hd * DH
            lg = _log_gamma(hd)
            qf = z[:, lo:lo + DH]
            kf = z[:, W + lo:W + lo + DH]
            q = (qf * cosf + pltpu.roll(qf, DH // 2, 1) * sinf).astype(BF16)
            k_t = ((kf * cosf + pltpu.roll(kf, DH // 2, 1) * sinf) * DH ** -0.5).T
            v = z[:, 2 * W + lo:2 * W + lo + DH].astype(BF16)
            decay = jnp.where(rel >= 0.0, jnp.exp(rel * lg), 0.0)
            s = _dot(q, k_t.astype(BF16)) * decay
            st = state_ref[hd]
            out = _dot(s.astype(BF16), v) + jnp.exp((idx_col + 1.0) * lg) * _dot(q, st.astype(BF16))
            k_w = (k_t * jnp.exp((L - 1.0 - idx_row) * lg)).astype(BF16)
            state_ref[hd] = math.exp(L * lg) * st + _dot(k_w, v)
            g = z[:, 3 * W + lo:3 * W + lo + DH]
            y = (g * jax.nn.sigmoid(g)) * _head_norm(out, gain_ref[:, lo:lo + DH])
            o_ref[pl.ds(r0, L), lo:lo + DH] = y.astype(BF16)
        return carry

    lax.fori_loop(0, seq // L, step, 0)


def _retention(h2d, w, layer, b, cos_t, sin_t, gain, batch, seq):
    d = h2d.shape[1]
    n = w.shape[2]
    return pl.pallas_call(
        _retention_kernel,
        grid=(batch,),
        in_specs=[
            pl.BlockSpec((seq, d), lambda i: (i, 0)),
            _layer_spec((d, n), layer),
            _const_spec((1, n)),
            pl.BlockSpec((None, seq, DH), lambda i: (i, 0, 0)),
            pl.BlockSpec((None, seq, DH), lambda i: (i, 0, 0)),
            _const_spec((1, BRANCH_W)),
        ],
        out_specs=pl.BlockSpec((seq, BRANCH_W), lambda i: (i, 0)),
        out_shape=jax.ShapeDtypeStruct((batch * seq, BRANCH_W), BF16),
        scratch_shapes=[pltpu.VMEM((HEADS, DH, DH), F32)],
        compiler_params=_params(("arbitrary",)),
        name="retention",
    )(h2d, w, b, cos_t, sin_t, gain)


def _fox_kernel(h_ref, w_ref, b_ref, o_ref, kt_ref, v_ref, f_ref, fc_ref, q_ref, acc_ref):
    seq = h_ref.shape[0]
    L = CHUNK
    W = HEADS * DH
    fc_ref[...] = jnp.zeros_like(fc_ref)

    def step(i, carry):
        r0 = pl.multiple_of(i * L, L)
        z = _dot(h_ref[pl.ds(r0, L), :], w_ref[...]) + b_ref[...]
        causal = _causal_mask(L)
        f_pre = z[:, 3 * W:3 * W + LANES].T[0:SUBLANES, :]
        cum_f = _dot_f32(_log_sigmoid(f_pre), _prefix_sum_matrix(L)) + fc_ref[:, 0:1]
        f_ref[i] = cum_f
        fc_ref[...] = jnp.broadcast_to(cum_f[:, L - 1:L], (SUBLANES, LANES))
        ones_blk = _ones_column_block(L)
        row_max = []
        for hd in range(HEADS):
            lo = hd * DH
            k_t = z[:, W + lo:W + lo + DH].T.astype(BF16)
            v_aug = jnp.concatenate(
                [z[:, 2 * W + lo:2 * W + lo + DH], ones_blk], axis=1).astype(BF16)
            kt_ref[hd, i] = k_t
            v_ref[hd, i] = v_aug
            q = (z[:, lo:lo + DH] * DH ** -0.5).astype(BF16)
            q_ref[hd] = q
            s = jnp.where(causal, _dot(q, k_t) - cum_f[hd:hd + 1, :], -jnp.inf)
            m = jnp.max(s, axis=1, keepdims=True)
            acc_ref[hd] = _dot(jnp.exp(s - m).astype(BF16), v_aug)
            row_max.append(m)

        def kv_step(j, ms):
            f_j = f_ref[j]
            out = []
            for hd in range(HEADS):
                sj = _dot(q_ref[hd], kt_ref[hd, j]) - f_j[hd:hd + 1, :]
                m_new = jnp.maximum(ms[hd], jnp.max(sj, axis=1, keepdims=True))
                pj = jnp.exp(sj - m_new).astype(BF16)
                acc_ref[hd] = jnp.exp(ms[hd] - m_new) * acc_ref[hd] + _dot(pj, v_ref[hd, j])
                out.append(m_new)
            return tuple(out)

        lax.fori_loop(0, i, kv_step, tuple(row_max))
        for hd in range(HEADS):
            acc = acc_ref[hd]
            o_ref[pl.ds(r0, L), hd * DH:(hd + 1) * DH] = (
                acc[:, :DH] / acc[:, DH:DH + 1]).astype(BF16)
        return carry

    lax.fori_loop(0, seq // L, step, 0)


def _fox(h2d, w, layer, b, batch, seq):
    d = h2d.shape[1]
    n = w.shape[2]
    nc = seq // CHUNK
    return pl.pallas_call(
        _fox_kernel,
        grid=(batch,),
        in_specs=[
            pl.BlockSpec((seq, d), lambda i: (i, 0)),
            _layer_spec((d, n), layer),
            _const_spec((1, n)),
        ],
        out_specs=pl.BlockSpec((seq, BRANCH_W), lambda i: (i, 0)),
        out_shape=jax.ShapeDtypeStruct((batch * seq, BRANCH_W), BF16),
        scratch_shapes=[
            pltpu.VMEM((HEADS, nc, DH, CHUNK), BF16),
            pltpu.VMEM((HEADS, nc, CHUNK, 2 * DH), BF16),
            pltpu.VMEM((nc, SUBLANES, CHUNK), F32),
            pltpu.VMEM((SUBLANES, LANES), F32),
            pltpu.VMEM((HEADS, CHUNK, DH), BF16),
            pltpu.VMEM((HEADS, CHUNK, 2 * DH), F32),
        ],
        compiler_params=_params(("arbitrary",)),
        name="forgetting_attention",
    )(h2d, w, b)


def _merge_kernel(h_ref, ym_ref, yl_ref, yr_ref, yf_ref, x_ref, mod_ref, wg_ref, bg_ref,
                  wbr_ref, wo_ref, bo_ref, lng_ref, lnb_ref, o_ref, hn_ref, *, alpha):
    d = x_ref.shape[1]
    h = h_ref[...]
    merged = None
    for n, y_ref in enumerate((ym_ref, yl_ref, yr_ref, yf_ref)):
        gate = jax.nn.sigmoid(_dot(h, wg_ref[:, n * d:(n + 1) * d]) + bg_ref[:, n * d:(n + 1) * d])
        term = gate * _dot(y_ref[...], wbr_ref[n])
        merged = term if merged is None else merged + term
    y = _dot(merged.astype(BF16), wo_ref[...]) + bo_ref[...]
    res = alpha * x_ref[...] + (1.0 + mod_ref[2:3, :]) * y
    x_new = _layernorm(res) * lng_ref[...] + lnb_ref[...]
    o_ref[...] = x_new
    hn_ref[...] = _ln_modulated(x_new, mod_ref, shift_row=3, scale_row=4)


def _merge(h2d, ys, x2d, mod_l, wg, layer, bg, wbr, wo, bo, lng, lnb, seq, alpha):
    tokens, d = x2d.shape
    tm = ROW_TILE
    tiles_per_seq = seq // tm
    row = lambda i: (i, 0)
    return pl.pallas_call(
        functools.partial(_merge_kernel, alpha=alpha),
        grid=(tokens // tm,),
        in_specs=[
            pl.BlockSpec((tm, d), row),
            pl.BlockSpec((tm, BRANCH_W), row),
            pl.BlockSpec((tm, BRANCH_W), row),
            pl.BlockSpec((tm, BRANCH_W), row),
            pl.BlockSpec((tm, BRANCH_W), row),
            pl.BlockSpec((tm, d), row),
            pl.BlockSpec((None, 6, d), lambda i: (i // tiles_per_seq, 0, 0)),
            _layer_spec((d, N_BRANCH * d), layer),
            _const_spec((1, N_BRANCH * d)),
            _const_spec((N_BRANCH, BRANCH_W, d)),
            _const_spec((d, d)),
            _const_spec((1, d)),
            _const_spec((1, d)),
            _const_spec((1, d)),
        ],
        out_specs=[pl.BlockSpec((tm, d), row), pl.BlockSpec((tm, d), row)],
        out_shape=[jax.ShapeDtypeStruct((tokens, d), F32),
                   jax.ShapeDtypeStruct((tokens, d), BF16)],
        compiler_params=_params(("parallel",)),
        name="merge_out_proj",
    )(h2d, *ys, x2d, mod_l, wg, bg, wbr, wo, bo, lng, lnb)


def _mlp_kernel(h_ref, x_ref, mod_ref, w1_ref, b1_ref, w2_ref, b2_ref, lng_ref, lnb_ref,
                *rest, alpha, ff_tile, emit_next):
    d_ff = w1_ref.shape[1]
    h = h_ref[...]
    acc = None
    for c in range(d_ff // ff_tile):
        cols = slice(c * ff_tile, (c + 1) * ff_tile)
        a = jnp.maximum(_dot(h, w1_ref[:, cols]) + b1_ref[:, cols], 0.0)
        part = _dot((a * a).astype(BF16), w2_ref[cols, :])
        acc = part if acc is None else acc + part
    y = acc + b2_ref[...]
    res = alpha * x_ref[...] + (1.0 + mod_ref[5:6, :]) * y
    x_new = _layernorm(res) * lng_ref[...] + lnb_ref[...]
    if emit_next:
        next_mod_ref, o_ref, hn_ref = rest
        hn_ref[...] = _ln_modulated(x_new, next_mod_ref, shift_row=0, scale_row=1)
    else:
        (o_ref,) = rest
    o_ref[...] = x_new


def _mlp(h2d, x2d, mod_l, next_mod, w1, b1, w2, b2, lng, lnb, seq, alpha):
    tokens, d = x2d.shape
    d_ff = w1.shape[1]
    tm = ROW_TILE
    tiles_per_seq = seq // tm
    row = lambda i: (i, 0)
    mod_spec = pl.BlockSpec((None, 6, d), lambda i: (i // tiles_per_seq, 0, 0))
    emit_next = next_mod is not None
    in_specs = [
        pl.BlockSpec((tm, d), row),
        pl.BlockSpec((tm, d), row),
        mod_spec,
        _const_spec((d, d_ff)),
        _const_spec((1, d_ff)),
        _const_spec((d_ff, d)),
        _const_spec((1, d)),
        _const_spec((1, d)),
        _const_spec((1, d)),
    ]
    args = [h2d, x2d, mod_l, w1, b1, w2, b2, lng, lnb]
    out_specs = [pl.BlockSpec((tm, d), row)]
    out_shape = [jax.ShapeDtypeStruct((tokens, d), F32)]
    if emit_next:
        in_specs.append(mod_spec)
        args.append(next_mod)
        out_specs.append(pl.BlockSpec((tm, d), row))
        out_shape.append(jax.ShapeDtypeStruct((tokens, d), BF16))
    outs = pl.pallas_call(
        functools.partial(_mlp_kernel, alpha=alpha, ff_tile=d, emit_next=emit_next),
        grid=(tokens // tm,),
        in_specs=in_specs,
        out_specs=out_specs,
        out_shape=out_shape,
        compiler_params=_params(("parallel",)),
        name="sq_relu_mlp",
    )(*args)
    return (outs[0], outs[1]) if emit_next else (outs[0], None)


_PACK_GROUPS = (
    (("m_q", "m_k", "m_v", "m_o"), ("m_i", "m_f")),
    (("l_x", "l_g"), ()),
    (("r_q", "r_k", "r_v", "r_g"), ()),
    (("f_q", "f_k", "f_v"), ("f_f",)),
    (("gate",), ()),
)
PACK_ROWS = 128


def _group_width(seg, names, gate_names):
    return sum(seg[n][1] - seg[n][0] for n in names) + (LANES if gate_names else 0)


def _pack_kernel(w_ref, *o_refs, seg):
    for o_ref, (names, gate_names) in zip(o_refs, _PACK_GROUPS):
        col = 0
        for n in names:
            a, b = seg[n]
            o_ref[:, col:col + (b - a)] = w_ref[:, a:b].astype(BF16)
            col += b - a
        if gate_names:
            o_ref[:, col:col + LANES] = jnp.zeros((o_ref.shape[0], LANES), BF16)
            for n in gate_names:
                a, b = seg[n]
                o_ref[:, col:col + (b - a)] = w_ref[:, a:b].astype(BF16)
                col += b - a


def _pack_weights(w_in, seg):
    depth, d, n_in = w_in.shape
    widths = [_group_width(seg, names, gates) for names, gates in _PACK_GROUPS]
    return pl.pallas_call(
        functools.partial(_pack_kernel, seg=seg),
        grid=(depth, d // PACK_ROWS),
        in_specs=[pl.BlockSpec((None, PACK_ROWS, n_in), lambda l, i: (l, i, 0))],
        out_specs=[pl.BlockSpec((None, PACK_ROWS, w), lambda l, i: (l, i, 0)) for w in widths],
        out_shape=[jax.ShapeDtypeStruct((depth, d, w), BF16) for w in widths],
        compiler_params=_params(("parallel", "parallel")),
        name="pack_in_proj",
    )(w_in)


def _pack_bias(b_in_l, seg, names, gate_names):
    bs = [b_in_l[seg[n][0]:seg[n][1]] for n in names]
    if gate_names:
        gb = jnp.concatenate([b_in_l[seg[n][0]:seg[n][1]] for n in gate_names])
        bs.append(jnp.pad(gb, (0, LANES - gb.shape[0])))
    return jnp.concatenate(bs)[None, :]


def _block_diag(w):
    nb, bd, _ = w.shape
    eye = jnp.eye(nb, dtype=w.dtype)
    return (eye[:, None, :, None] * w[:, :, None, :]).reshape(nb * bd, nb * bd)


def kernel(x, c, positions, w_ada, b_ada, w_in, b_in, m_norm, conv_w, conv_b, lru_wa, lru_ba,
           lru_wx, lru_bx, lru_lam, r_norm, w_br, w_out, b_out, ln1_g, ln1_b, w_ff1, b_ff1,
           w_ff2, b_ff2, ln2_g, ln2_b):
    batch, seq, d = x.shape
    depth = w_ada.shape[0]
    assert seq % CHUNK == 0 and seq % (2 * ROW_TILE) == 0 and d % LANES == 0
    alpha = (2 * depth) ** 0.25
    seg = _segment_slices(d)

    mod = _modulation(c, w_ada, b_ada).reshape(depth, batch, 6, d)
    cos_t, sin_t = _rope_tables(positions)
    x2d = x.reshape(batch * seq, d)
    packed = _pack_weights(w_in, seg)
    row = lambda v: v[None, :]
    w_m, w_l, w_r, w_f, w_g = packed
    h1 = _ln_modulate(x2d, mod[0], seq, shift_row=0, scale_row=1)
    for l in range(depth):
        b_m, b_l, b_r, b_f, b_g = (_pack_bias(b_in[l], seg, *grp) for grp in _PACK_GROUPS)
        y_m = _mlstm(h1, w_m, l, b_m, row(m_norm[l]), batch, seq)
        y_l = _rglru(h1, w_l, l, b_l, conv_w[l], row(conv_b[l]),
                     _block_diag(lru_wa[l]).astype(BF16), row(lru_ba[l]),
                     _block_diag(lru_wx[l]).astype(BF16), row(lru_bx[l]),
                     row(lru_lam[l]), batch, seq)
        y_r = _retention(h1, w_r, l, b_r, cos_t, sin_t, row(r_norm[l]), batch, seq)
        y_f = _fox(h1, w_f, l, b_f, batch, seq)
        x2d, h2 = _merge(h1, (y_m, y_l, y_r, y_f), x2d, mod[l], w_g, l, b_g,
                         w_br[l].astype(BF16), w_out[l].astype(BF16), row(b_out[l]),
                         row(ln1_g[l]), row(ln1_b[l]), seq, alpha)
        next_mod = mod[l + 1] if l + 1 < depth else None
        x2d, h1 = _mlp(h2, x2d, mod[l], next_mod, w_ff1[l].astype(BF16), row(b_ff1[l]),
                       w_ff2[l].astype(BF16), row(b_ff2[l]), row(ln2_g[l]), row(ln2_b[l]),
                       seq, alpha)
    return x2d.reshape(batch, seq, d)
```

```python
import functools
import math

import jax
import jax.numpy as jnp
from jax import lax
from jax.experimental import pallas as pl
from jax.experimental.pallas import tpu as pltpu

F32 = jnp.float32
BF16 = jnp.bfloat16

LN_EPS = 1e-5
HEADS = 4
DH = 128
BRANCH_W = HEADS * DH
LRU_BLOCKS = 8
LRU_C = 8.0
CONV_W = 4
ROPE_BASE = 10000.0
N_BRANCH = 4

LANES = 128
SUBLANES = 8
CHUNK = 256
ROW_TILE = 512
MERGE_SPLIT = 2
KV_GROUP = 2
VMEM_LIMIT_BYTES = 48 * 1024 * 1024

_SEGMENTS = (
    ("m_q", BRANCH_W), ("m_k", BRANCH_W), ("m_v", BRANCH_W), ("m_o", BRANCH_W),
    ("m_i", HEADS), ("m_f", HEADS),
    ("l_x", BRANCH_W), ("l_g", BRANCH_W),
    ("r_q", BRANCH_W), ("r_k", BRANCH_W), ("r_v", BRANCH_W), ("r_g", BRANCH_W),
    ("f_q", BRANCH_W), ("f_k", BRANCH_W), ("f_v", BRANCH_W), ("f_f", HEADS),
    ("gate", None),
)


def _segment_slices(d_model):
    out, acc = {}, 0
    for name, width in _SEGMENTS:
        width = N_BRANCH * d_model if width is None else width
        out[name] = (acc, acc + width)
        acc += width
    return out


def _dot(a, b):
    return jnp.dot(a, b, preferred_element_type=F32)


def _dot_f32(a, b):
    return jnp.dot(a, b, preferred_element_type=F32, precision=lax.Precision.HIGHEST)


def _layernorm(x):
    mu = jnp.mean(x, axis=-1, keepdims=True)
    xc = x - mu
    var = jnp.mean(xc * xc, axis=-1, keepdims=True)
    return xc * lax.rsqrt(var + LN_EPS)


def _log_sigmoid(x):
    return -(jnp.maximum(-x, 0.0) + jnp.log1p(jnp.exp(-jnp.abs(x))))


def _softplus(x):
    return jnp.maximum(x, 0.0) + jnp.log1p(jnp.exp(-jnp.abs(x)))


def _gelu_tanh(x):
    return x * (0.5 * (1.0 + jnp.tanh(math.sqrt(2.0 / math.pi) * (x + 0.044715 * (x * x * x)))))


def _causal_mask(n):
    row = lax.broadcasted_iota(jnp.int32, (n, n), 0)
    col = lax.broadcasted_iota(jnp.int32, (n, n), 1)
    return row >= col


def _prefix_sum_matrix(n):
    row = lax.broadcasted_iota(jnp.int32, (n, n), 0)
    col = lax.broadcasted_iota(jnp.int32, (n, n), 1)
    return jnp.where(row <= col, 1.0, 0.0).astype(F32)


def _ones_column_block(n):
    lane = lax.broadcasted_iota(jnp.int32, (n, LANES), 1)
    return jnp.where(lane == 0, 1.0, 0.0).astype(F32)


def _head_norm(y, gain):
    mu = jnp.mean(y, axis=-1, keepdims=True)
    yc = y - mu
    var = jnp.mean(yc * yc, axis=-1, keepdims=True)
    return yc * lax.rsqrt(var + LN_EPS) * gain


def _const_spec(shape):
    zeros = (0,) * len(shape)
    return pl.BlockSpec(shape, lambda *_: zeros, pipeline_mode=pl.Buffered(1))


def _layer_spec(shape, layer):
    idx = (layer,) + (0,) * len(shape)
    return pl.BlockSpec((None,) + tuple(shape), lambda *_: idx, pipeline_mode=pl.Buffered(1))


def _params(semantics):
    return pltpu.CompilerParams(
        dimension_semantics=semantics, vmem_limit_bytes=VMEM_LIMIT_BYTES)


def _mod_kernel(c_ref, w_ref, b_ref, o_ref):
    c = c_ref[...]
    cond = c * jax.nn.sigmoid(c)
    o_ref[...] = _dot_f32(cond, w_ref[...]) + b_ref[...]


def _modulation(c, w_ada, b_ada):
    depth, d, six_d = w_ada.shape
    batch = c.shape[0]
    n_tiles = six_d // d
    return pl.pallas_call(
        _mod_kernel,
        grid=(depth, n_tiles),
        in_specs=[
            pl.BlockSpec((batch, d), lambda l, j: (0, 0)),
            pl.BlockSpec((None, d, d), lambda l, j: (l, 0, j)),
            pl.BlockSpec((None, 1, d), lambda l, j: (l, 0, j)),
        ],
        out_specs=pl.BlockSpec((None, batch, d), lambda l, j: (l, 0, j)),
        out_shape=jax.ShapeDtypeStruct((depth, batch, six_d), F32),
        compiler_params=_params(("parallel", "parallel")),
        name="adaln_modulation",
    )(c, w_ada, b_ada.reshape(depth, 1, six_d))


def _rope_kernel(pos_ref, invf_ref, cos_ref, sin_ref):
    ang = invf_ref[...] * pos_ref[...].astype(F32)
    c = jnp.cos(ang)
    s = jnp.sin(ang)
    cos_ref[...] = jnp.concatenate([c, c], axis=0).T
    sin_ref[...] = jnp.concatenate([-s, s], axis=0).T


def _rope_tables(positions):
    batch, seq = positions.shape
    inv_freq = ROPE_BASE ** (-jnp.arange(0, DH, 2, dtype=F32) / DH)
    out = jax.ShapeDtypeStruct((batch, seq, DH), F32)
    return pl.pallas_call(
        _rope_kernel,
        grid=(batch,),
        in_specs=[
            pl.BlockSpec((None, 1, seq), lambda b: (b, 0, 0)),
            pl.BlockSpec((DH // 2, 1), lambda b: (0, 0)),
        ],
        out_specs=[pl.BlockSpec((None, seq, DH), lambda b: (b, 0, 0))] * 2,
        out_shape=[out, out],
        compiler_params=_params(("parallel",)),
        name="rope_tables",
    )(positions.reshape(batch, 1, seq), inv_freq.reshape(DH // 2, 1))


def _ln_modulated(x, mod_ref, shift_row, scale_row):
    scale = mod_ref[scale_row:scale_row + 1, :]
    shift = mod_ref[shift_row:shift_row + 1, :]
    return (_layernorm(x) * (1.0 + scale) + shift).astype(BF16)


def _ln_mod_kernel(x_ref, mod_ref, o_ref, *, shift_row, scale_row):
    o_ref[...] = _ln_modulated(x_ref[...], mod_ref, shift_row, scale_row)


def _ln_modulate(x2d, mod_l, seq, shift_row, scale_row):
    tokens, d = x2d.shape
    tm = 2 * ROW_TILE
    tiles_per_seq = seq // tm
    return pl.pallas_call(
        functools.partial(_ln_mod_kernel, shift_row=shift_row, scale_row=scale_row),
        grid=(tokens // tm,),
        in_specs=[
            pl.BlockSpec((tm, d), lambda i: (i, 0)),
            pl.BlockSpec((None, 6, d), lambda i: (i // tiles_per_seq, 0, 0)),
        ],
        out_specs=pl.BlockSpec((tm, d), lambda i: (i, 0)),
        out_shape=jax.ShapeDtypeStruct((tokens, d), BF16),
        compiler_params=_params(("parallel",)),
        name="ln_modulate",
    )(x2d, mod_l)


def _mlstm_kernel(h_ref, w_ref, b_ref, gain_ref, o_ref, state_ref, mc_ref, bc_ref, z_ref):
    seq = h_ref.shape[0]
    L = CHUNK
    nc = seq // L
    qkvo = HEADS * DH
    state_ref[...] = jnp.zeros_like(state_ref)
    mc_ref[...] = jnp.zeros_like(mc_ref)
    bc_ref[...] = jnp.zeros_like(bc_ref)

    def project(i):
        r0 = pl.multiple_of(i * L, L)
        z_ref[...] = _dot(h_ref[pl.ds(r0, L), :], w_ref[...]) + b_ref[...]

    def chunk(i, project_next):
        r0 = pl.multiple_of(i * L, L)
        z = z_ref[...]
        causal = _causal_mask(L)
        gates = z[:, 4 * qkvo:4 * qkvo + LANES].T[0:2 * HEADS, :]
        log_f = _log_sigmoid(gates)
        cum_f = _dot_f32(log_f, _prefix_sum_matrix(L))
        ones_blk = _ones_column_block(L)
        heads = []
        for hd in range(HEADS):
            lo = hd * DH
            heads.append((
                z[:, lo:lo + DH].astype(BF16),
                (z[:, qkvo + lo:qkvo + lo + DH] * DH ** -0.5).T,
                jnp.concatenate(
                    [z[:, 2 * qkvo + lo:2 * qkvo + lo + DH], ones_blk], axis=1).astype(BF16),
                jax.nn.sigmoid(z[:, 3 * qkvo + lo:3 * qkvo + lo + DH])))
        if project_next:
            project(i + 1)
        for hd, (q, k_t, v_aug, o_gate) in enumerate(heads):
            lo = hd * DH
            mc = mc_ref[hd][:, 0:1]
            bc = bc_ref[hd][:, 0:1]
            lf_row = log_f[HEADS + hd:HEADS + hd + 1, :]
            b_row = cum_f[HEADS + hd:HEADS + hd + 1, :] + bc
            c_row = gates[hd:hd + 1, :] - b_row
            d0 = jnp.where(causal, c_row, -jnp.inf)
            m_t = jnp.maximum(jnp.max(d0, axis=1, keepdims=True), mc)
            p = jnp.exp(d0 - m_t)
            b_col = jnp.sum(jnp.where(causal, lf_row, 0.0), axis=1, keepdims=True) + bc
            s = _dot(q, k_t.astype(BF16))
            intra = _dot((s * p).astype(BF16), v_aug)
            st = state_ref[hd]
            inter = _dot(q, st.astype(BF16))
            tot = jnp.exp(mc - m_t) * inter + intra
            den = tot[:, DH:DH + 1]
            hh = tot[:, :DH] / jnp.maximum(jnp.abs(den), jnp.exp(-(b_col + m_t)))
            y = _head_norm(o_gate * hh, gain_ref[:, lo:lo + DH])
            o_ref[pl.ds(r0, L), lo:lo + DH] = y.astype(BF16)
            mc_new = m_t[L - 1:L, :]
            k_w = (k_t * jnp.exp(c_row - mc_new)).astype(BF16)
            state_ref[hd] = jnp.exp(mc - mc_new) * st + _dot(k_w, v_aug)
            mc_ref[hd] = jnp.broadcast_to(mc_new, (1, LANES))
            bc_ref[hd] = jnp.broadcast_to(b_row[:, L - 1:L], (1, LANES))

    project(0)

    def step(i, carry):
        chunk(i, project_next=True)
        return carry

    lax.fori_loop(0, nc - 1, step, 0)
    chunk(nc - 1, project_next=False)


def _mlstm(h2d, w, layer, b, gain, batch, seq):
    d = h2d.shape[1]
    n = w.shape[2]
    return pl.pallas_call(
        _mlstm_kernel,
        grid=(batch,),
        in_specs=[
            pl.BlockSpec((seq, d), lambda i: (i, 0)),
            _layer_spec((d, n), layer),
            _const_spec((1, n)),
            _const_spec((1, BRANCH_W)),
        ],
        out_specs=pl.BlockSpec((seq, BRANCH_W), lambda i: (i, 0)),
        out_shape=jax.ShapeDtypeStruct((batch * seq, BRANCH_W), BF16),
        scratch_shapes=[
            pltpu.VMEM((HEADS, DH, 2 * DH), F32),
            pltpu.VMEM((HEADS, 1, LANES), F32),
            pltpu.VMEM((HEADS, 1, LANES), F32),
            pltpu.VMEM((CHUNK, n), F32),
        ],
        compiler_params=_params(("arbitrary",)),
        name="mlstm",
    )(h2d, w, b, gain)


def _rglru_kernel(h_ref, w_ref, b_ref, cw_ref, cb_ref, wa_ref, ba_ref, wx_ref, bx_ref,
                  lam_ref, o_ref, xpad_ref, hbuf_ref, hcar_ref):
    seq = h_ref.shape[0]
    L = CHUNK
    R = BRANCH_W
    pad = SUBLANES
    xpad_ref[0:pad, :] = jnp.zeros((pad, R), F32)
    hcar_ref[...] = jnp.zeros_like(hcar_ref)

    def step(i, carry):
        r0 = pl.multiple_of(i * L, L)
        z = _dot(h_ref[pl.ds(r0, L), :], w_ref[...]) + b_ref[...]
        xpad_ref[pad:pad + L, :] = z[:, :R]
        xp = xpad_ref[...]
        xc = cb_ref[...]
        for k in range(CONV_W):
            off = pad - (CONV_W - 1) + k
            xc = xc + cw_ref[k:k + 1, :] * xp[off:off + L, :]
        xpad_ref[0:pad, :] = xp[L:L + pad, :]
        xcb = xc.astype(BF16)
        r = jax.nn.sigmoid(_dot(xcb, wa_ref[...]) + ba_ref[...])
        ig = jax.nn.sigmoid(_dot(xcb, wx_ref[...]) + bx_ref[...])
        log_a = (-LRU_C) * r * _softplus(-lam_ref[...])
        a = jnp.exp(log_a)
        th = jnp.tanh(log_a)
        u = jnp.sqrt(-2.0 * th / (1.0 - th)) * (ig * xc)
        groups = L // SUBLANES
        a = a.reshape(groups, SUBLANES, R)
        u = u.reshape(groups, SUBLANES, R)
        sub = lax.broadcasted_iota(jnp.int32, (groups, SUBLANES, R), 1)
        for sft in (1, 2, 4):
            ok = sub >= sft
            u = jnp.where(ok, a * pltpu.roll(u, sft, 1) + u, u)
            a = jnp.where(ok, a * pltpu.roll(a, sft, 1), a)
        hprev = hcar_ref[...]
        for g in range(groups):
            rows = slice(g * SUBLANES, (g + 1) * SUBLANES)
            hg = a[g] * hprev + u[g]
            hbuf_ref[rows, :] = hg
            hprev = hg[SUBLANES - 1:SUBLANES, :]
        hcar_ref[...] = hprev
        o_ref[pl.ds(r0, L), :] = (hbuf_ref[...] * _gelu_tanh(z[:, R:2 * R])).astype(BF16)
        return carry

    lax.fori_loop(0, seq // L, step, 0)


def _rglru(h2d, w, layer, b, conv_w, conv_b, wa, ba, wx, bx, lam, batch, seq):
    d = h2d.shape[1]
    R = BRANCH_W
    return pl.pallas_call(
        _rglru_kernel,
        grid=(batch,),
        in_specs=[
            pl.BlockSpec((seq, d), lambda i: (i, 0)),
            _layer_spec((d, 2 * R), layer),
            _const_spec((1, 2 * R)),
            _const_spec((CONV_W, R)),
            _const_spec((1, R)),
            _const_spec((R, R)),
            _const_spec((1, R)),
            _const_spec((R, R)),
            _const_spec((1, R)),
            _const_spec((1, R)),
        ],
        out_specs=pl.BlockSpec((seq, R), lambda i: (i, 0)),
        out_shape=jax.ShapeDtypeStruct((batch * seq, R), BF16),
        scratch_shapes=[
            pltpu.VMEM((CHUNK + SUBLANES, R), F32),
            pltpu.VMEM((CHUNK, R), F32),
            pltpu.VMEM((1, R), F32),
        ],
        compiler_params=_params(("arbitrary",)),
        name="rglru",
    )(h2d, w, b, conv_w, conv_b, wa, ba, wx, bx, lam)


def _log_gamma(hd):
    return math.log1p(-(2.0 ** (-5.0 - hd)))


def _retention_kernel(h_ref, w_ref, b_ref, cos_ref, sin_ref, gain_ref, o_ref, state_ref,
                      decay_ref, z_ref):
    seq = h_ref.shape[0]
    L = CHUNK
    nc = seq // L
    W = HEADS * DH
    state_ref[...] = jnp.zeros_like(state_ref)
    row = lax.broadcasted_iota(jnp.int32, (L, L), 0)
    col = lax.broadcasted_iota(jnp.int32, (L, L), 1)
    rel = (row - col).astype(F32)
    for hd in range(HEADS):
        decay_ref[hd] = jnp.where(rel >= 0.0, jnp.exp(rel * _log_gamma(hd)), 0.0)

    def project(i):
        r0 = pl.multiple_of(i * L, L)
        z_ref[...] = _dot(h_ref[pl.ds(r0, L), :], w_ref[...]) + b_ref[...]

    def chunk(i, project_next):
        r0 = pl.multiple_of(i * L, L)
        z = z_ref[...]
        cosf = cos_ref[pl.ds(r0, L), :]
        sinf = sin_ref[pl.ds(r0, L), :]
        idx_col = lax.broadcasted_iota(jnp.int32, (L, 1), 0).astype(F32)
        idx_row = lax.broadcasted_iota(jnp.int32, (1, L), 1).astype(F32)
        heads = []
        for hd in range(HEADS):
            lo = hd * DH
            qf = z[:, lo:lo + DH]
            kf = z[:, W + lo:W + lo + DH]
            heads.append((
                (qf * cosf + pltpu.roll(qf, DH // 2, 1) * sinf).astype(BF16),
                ((kf * cosf + pltpu.roll(kf, DH // 2, 1) * sinf) * DH ** -0.5).T,
                z[:, 2 * W + lo:2 * W + lo + DH].astype(BF16),
                z[:, 3 * W + lo:3 * W + lo + DH]))
        if project_next:
            project(i + 1)
        for hd, (q, k_t, v, g) in enumerate(heads):
            lo = hd * DH
            lg = _log_gamma(hd)
            s = _dot(q, k_t.astype(BF16)) * decay_ref[hd]
            st = state_ref[hd]
            out = _dot(s.astype(BF16), v) + jnp.exp((idx_col + 1.0) * lg) * _dot(q, st.astype(BF16))
            k_w = (k_t * jnp.exp((L - 1.0 - idx_row) * lg)).astype(BF16)
            state_ref[hd] = math.exp(L * lg) * st + _dot(k_w, v)
            y = (g * jax.nn.sigmoid(g)) * _head_norm(out, gain_ref[:, lo:lo + DH])
            o_ref[pl.ds(r0, L), lo:lo + DH] = y.astype(BF16)

    project(0)

    def step(i, carry):
        chunk(i, project_next=True)
        return carry

    lax.fori_loop(0, nc - 1, step, 0)
    chunk(nc - 1, project_next=False)


def _retention(h2d, w, layer, b, cos_t, sin_t, gain, batch, seq):
    d = h2d.shape[1]
    n = w.shape[2]
    return pl.pallas_call(
        _retention_kernel,
        grid=(batch,),
        in_specs=[
            pl.BlockSpec((seq, d), lambda i: (i, 0)),
            _layer_spec((d, n), layer),
            _const_spec((1, n)),
            pl.BlockSpec((None, seq, DH), lambda i: (i, 0, 0)),
            pl.BlockSpec((None, seq, DH), lambda i: (i, 0, 0)),
            _const_spec((1, BRANCH_W)),
        ],
        out_specs=pl.BlockSpec((seq, BRANCH_W), lambda i: (i, 0)),
        out_shape=jax.ShapeDtypeStruct((batch * seq, BRANCH_W), BF16),
        scratch_shapes=[
            pltpu.VMEM((HEADS, DH, DH), F32),
            pltpu.VMEM((HEADS, CHUNK, CHUNK), F32),
            pltpu.VMEM((CHUNK, n), F32),
        ],
        compiler_params=_params(("arbitrary",)),
        name="retention",
    )(h2d, w, b, cos_t, sin_t, gain)


def _fox_kernel(h_ref, w_ref, b_ref, o_ref, kt_ref, v_ref, f_ref, fc_ref, q_ref, acc_ref):
    seq = h_ref.shape[0]
    L = CHUNK
    W = HEADS * DH
    fc_ref[...] = jnp.zeros_like(fc_ref)

    def step(i, carry):
        r0 = pl.multiple_of(i * L, L)
        z = _dot(h_ref[pl.ds(r0, L), :], w_ref[...]) + b_ref[...]
        causal = _causal_mask(L)
        f_pre = z[:, 3 * W:3 * W + LANES].T[0:SUBLANES, :]
        cum_f = _dot_f32(_log_sigmoid(f_pre), _prefix_sum_matrix(L)) + fc_ref[:, 0:1]
        f_ref[i] = cum_f
        fc_ref[...] = jnp.broadcast_to(cum_f[:, L - 1:L], (SUBLANES, LANES))
        ones_blk = _ones_column_block(L)
        row_max = []
        for hd in range(HEADS):
            lo = hd * DH
            k_t = z[:, W + lo:W + lo + DH].T.astype(BF16)
            v_aug = jnp.concatenate(
                [z[:, 2 * W + lo:2 * W + lo + DH], ones_blk], axis=1).astype(BF16)
            kt_ref[hd, i] = k_t
            v_ref[hd, i] = v_aug
            q = (z[:, lo:lo + DH] * DH ** -0.5).astype(BF16)
            q_ref[hd] = q
            s = jnp.where(causal, _dot(q, k_t) - cum_f[hd:hd + 1, :], -jnp.inf)
            m = jnp.max(s, axis=1, keepdims=True)
            acc_ref[hd] = _dot(jnp.exp(s - m).astype(BF16), v_aug)
            row_max.append(m)

        def kv_steps(width):
            def body(t, ms, first):
                js = [first + t * width + u for u in range(width)]
                fs = [f_ref[j] for j in js]
                out = []
                for hd in range(HEADS):
                    ss = [_dot(q_ref[hd], kt_ref[hd, j]) - f[hd:hd + 1, :]
                          for j, f in zip(js, fs)]
                    s_max = functools.reduce(jnp.maximum, ss)
                    m_new = jnp.maximum(ms[hd], jnp.max(s_max, axis=1, keepdims=True))
                    acc = jnp.exp(ms[hd] - m_new) * acc_ref[hd]
                    for j, s in zip(js, ss):
                        acc = acc + _dot(jnp.exp(s - m_new).astype(BF16), v_ref[hd, j])
                    acc_ref[hd] = acc
                    out.append(m_new)
                return tuple(out)
            return body

        n_groups = i // KV_GROUP
        ms = lax.fori_loop(
            0, n_groups, functools.partial(kv_steps(KV_GROUP), first=0), tuple(row_max))
        lax.fori_loop(
            0, i - n_groups * KV_GROUP,
            functools.partial(kv_steps(1), first=n_groups * KV_GROUP), ms)
        for hd in range(HEADS):
            acc = acc_ref[hd]
            o_ref[pl.ds(r0, L), hd * DH:(hd + 1) * DH] = (
                acc[:, :DH] / acc[:, DH:DH + 1]).astype(BF16)
        return carry

    lax.fori_loop(0, seq // L, step, 0)


def _fox(h2d, w, layer, b, batch, seq):
    d = h2d.shape[1]
    n = w.shape[2]
    nc = seq // CHUNK
    return pl.pallas_call(
        _fox_kernel,
        grid=(batch,),
        in_specs=[
            pl.BlockSpec((seq, d), lambda i: (i, 0)),
            _layer_spec((d, n), layer),
            _const_spec((1, n)),
        ],
        out_specs=pl.BlockSpec((seq, BRANCH_W), lambda i: (i, 0)),
        out_shape=jax.ShapeDtypeStruct((batch * seq, BRANCH_W), BF16),
        scratch_shapes=[
            pltpu.VMEM((HEADS, nc, DH, CHUNK), BF16),
            pltpu.VMEM((HEADS, nc, CHUNK, 2 * DH), BF16),
            pltpu.VMEM((nc, SUBLANES, CHUNK), F32),
            pltpu.VMEM((SUBLANES, LANES), F32),
            pltpu.VMEM((HEADS, CHUNK, DH), BF16),
            pltpu.VMEM((HEADS, CHUNK, 2 * DH), F32),
        ],
        compiler_params=_params(("arbitrary",)),
        name="forgetting_attention",
    )(h2d, w, b)


def _merge_kernel(h_ref, ym_ref, yl_ref, yr_ref, yf_ref, x_ref, mod_ref, wg_ref, bg_ref,
                  wbr_ref, wo_ref, bo_ref, lng_ref, lnb_ref, o_ref, hn_ref, *, alpha):
    tm, d = x_ref.shape
    for r0 in range(0, tm, tm // MERGE_SPLIT):
        rows = slice(r0, r0 + tm // MERGE_SPLIT)
        h = h_ref[rows, :]
        merged = None
        for n, y_ref in enumerate((ym_ref, yl_ref, yr_ref, yf_ref)):
            gate = jax.nn.sigmoid(
                _dot(h, wg_ref[:, n * d:(n + 1) * d]) + bg_ref[:, n * d:(n + 1) * d])
            term = gate * _dot(y_ref[rows, :], wbr_ref[n])
            merged = term if merged is None else merged + term
        y = _dot(merged.astype(BF16), wo_ref[...]) + bo_ref[...]
        res = alpha * x_ref[rows, :] + (1.0 + mod_ref[2:3, :]) * y
        x_new = _layernorm(res) * lng_ref[...] + lnb_ref[...]
        o_ref[rows, :] = x_new
        hn_ref[rows, :] = _ln_modulated(x_new, mod_ref, shift_row=3, scale_row=4)


def _merge(h2d, ys, x2d, mod_l, wg, layer, bg, wbr, wo, bo, lng, lnb, seq, alpha):
    tokens, d = x2d.shape
    tm = ROW_TILE
    tiles_per_seq = seq // tm
    row = lambda i: (i, 0)
    return pl.pallas_call(
        functools.partial(_merge_kernel, alpha=alpha),
        grid=(tokens // tm,),
        in_specs=[
            pl.BlockSpec((tm, d), row),
            pl.BlockSpec((tm, BRANCH_W), row),
            pl.BlockSpec((tm, BRANCH_W), row),
            pl.BlockSpec((tm, BRANCH_W), row),
            pl.BlockSpec((tm, BRANCH_W), row),
            pl.BlockSpec((tm, d), row),
            pl.BlockSpec((None, 6, d), lambda i: (i // tiles_per_seq, 0, 0)),
            _layer_spec((d, N_BRANCH * d), layer),
            _const_spec((1, N_BRANCH * d)),
            _const_spec((N_BRANCH, BRANCH_W, d)),
            _const_spec((d, d)),
            _const_spec((1, d)),
            _const_spec((1, d)),
            _const_spec((1, d)),
        ],
        out_specs=[pl.BlockSpec((tm, d), row), pl.BlockSpec((tm, d), row)],
        out_shape=[jax.ShapeDtypeStruct((tokens, d), F32),
                   jax.ShapeDtypeStruct((tokens, d), BF16)],
        compiler_params=_params(("parallel",)),
        name="merge_out_proj",
    )(h2d, *ys, x2d, mod_l, wg, bg, wbr, wo, bo, lng, lnb)


def _mlp_kernel(h_ref, x_ref, mod_ref, w1_ref, b1_ref, w2_ref, b2_ref, lng_ref, lnb_ref,
                *rest, alpha, ff_tile, emit_next):
    d_ff = w1_ref.shape[1]
    tm = x_ref.shape[0]
    if emit_next:
        next_mod_ref, o_ref, hn_ref = rest
    else:
        (o_ref,) = rest
    for r0 in range(0, tm, tm // MERGE_SPLIT):
        rows = slice(r0, r0 + tm // MERGE_SPLIT)
        h = h_ref[rows, :]
        acc = None
        for c in range(d_ff // ff_tile):
            cols = slice(c * ff_tile, (c + 1) * ff_tile)
            a = jnp.maximum(_dot(h, w1_ref[:, cols]) + b1_ref[:, cols], 0.0)
            part = _dot((a * a).astype(BF16), w2_ref[cols, :])
            acc = part if acc is None else acc + part
        y = acc + b2_ref[...]
        res = alpha * x_ref[rows, :] + (1.0 + mod_ref[5:6, :]) * y
        x_new = _layernorm(res) * lng_ref[...] + lnb_ref[...]
        o_ref[rows, :] = x_new
        if emit_next:
            hn_ref[rows, :] = _ln_modulated(x_new, next_mod_ref, shift_row=0, scale_row=1)


def _mlp(h2d, x2d, mod_l, next_mod, w1, b1, w2, b2, lng, lnb, seq, alpha):
    tokens, d = x2d.shape
    d_ff = w1.shape[1]
    tm = ROW_TILE
    tiles_per_seq = seq // tm
    row = lambda i: (i, 0)
    mod_spec = pl.BlockSpec((None, 6, d), lambda i: (i // tiles_per_seq, 0, 0))
    emit_next = next_mod is not None
    in_specs = [
        pl.BlockSpec((tm, d), row),
        pl.BlockSpec((tm, d), row),
        mod_spec,
        _const_spec((d, d_ff)),
        _const_spec((1, d_ff)),
        _const_spec((d_ff, d)),
        _const_spec((1, d)),
        _const_spec((1, d)),
        _const_spec((1, d)),
    ]
    args = [h2d, x2d, mod_l, w1, b1, w2, b2, lng, lnb]
    out_specs = [pl.BlockSpec((tm, d), row)]
    out_shape = [jax.ShapeDtypeStruct((tokens, d), F32)]
    if emit_next:
        in_specs.append(mod_spec)
        args.append(next_mod)
        out_specs.append(pl.BlockSpec((tm, d), row))
        out_shape.append(jax.ShapeDtypeStruct((tokens, d), BF16))
    outs = pl.pallas_call(
        functools.partial(_mlp_kernel, alpha=alpha, ff_tile=d, emit_next=emit_next),
        grid=(tokens // tm,),
        in_specs=in_specs,
        out_specs=out_specs,
        out_shape=out_shape,
        compiler_params=_params(("parallel",)),
        name="sq_relu_mlp",
    )(*args)
    return (outs[0], outs[1]) if emit_next else (outs[0], None)


_PACK_GROUPS = (
    (("m_q", "m_k", "m_v", "m_o"), ("m_i", "m_f")),
    (("l_x", "l_g"), ()),
    (("r_q", "r_k", "r_v", "r_g"), ()),
    (("f_q", "f_k", "f_v"), ("f_f",)),
    (("gate",), ()),
)
def _pack_kernel(w_ref, o_ref, *, n_blocks, last_valid):
    depth, d = w_ref.shape[1], w_ref.shape[2]
    lane = lax.broadcasted_iota(jnp.int32, (d, LANES), 1)
    n_valid = jnp.where(pl.program_id(0) == n_blocks - 1, last_valid, LANES)
    for l in range(depth):
        o_ref[l] = jnp.where(lane < n_valid, w_ref[:, l, :].T, 0.0).astype(BF16)


def _pack_group(w_cols, seg, names, gate_names):
    n_in, depth, d = w_cols.shape
    start = seg[names[0]][0]
    main = seg[names[-1]][1] - start
    assert main % LANES == 0
    gate_w = sum(seg[n][1] - seg[n][0] for n in gate_names)
    assert not gate_names or seg[gate_names[0]][0] == start + main
    n_blocks = main // LANES + (1 if gate_names else 0)
    return pl.pallas_call(
        functools.partial(_pack_kernel, n_blocks=n_blocks,
                          last_valid=gate_w if gate_names else LANES),
        grid=(n_blocks,),
        in_specs=[pl.BlockSpec((pl.Element(LANES), pl.Element(depth), pl.Element(d)),
                               lambda j: (start + j * LANES, 0, 0))],
        out_specs=pl.BlockSpec((depth, d, LANES), lambda j: (0, 0, j)),
        out_shape=jax.ShapeDtypeStruct((depth, d, n_blocks * LANES), BF16),
        compiler_params=_params(("parallel",)),
        name="pack_in_proj",
    )(w_cols)


def _pack_weights(w_in, seg):
    w_cols = jnp.transpose(w_in, (2, 0, 1))
    return [_pack_group(w_cols, seg, names, gates) for names, gates in _PACK_GROUPS]


def _pack_bias(b_in_l, seg, names, gate_names):
    bs = [b_in_l[seg[n][0]:seg[n][1]] for n in names]
    if gate_names:
        gb = jnp.concatenate([b_in_l[seg[n][0]:seg[n][1]] for n in gate_names])
        bs.append(jnp.pad(gb, (0, LANES - gb.shape[0])))
    return jnp.concatenate(bs)[None, :]


def _block_diag(w):
    nb, bd, _ = w.shape
    eye = jnp.eye(nb, dtype=w.dtype)
    return (eye[:, None, :, None] * w[:, :, None, :]).reshape(nb * bd, nb * bd)


def kernel(x, c, positions, w_ada, b_ada, w_in, b_in, m_norm, conv_w, conv_b, lru_wa, lru_ba,
           lru_wx, lru_bx, lru_lam, r_norm, w_br, w_out, b_out, ln1_g, ln1_b, w_ff1, b_ff1,
           w_ff2, b_ff2, ln2_g, ln2_b):
    batch, seq, d = x.shape
    depth = w_ada.shape[0]
    assert seq % CHUNK == 0 and seq % (2 * ROW_TILE) == 0 and d % LANES == 0
    alpha = (2 * depth) ** 0.25
    seg = _segment_slices(d)

    mod = _modulation(c, w_ada, b_ada).reshape(depth, batch, 6, d)
    cos_t, sin_t = _rope_tables(positions)
    x2d = x.reshape(batch * seq, d)
    packed = _pack_weights(w_in, seg)
    row = lambda v: v[None, :]
    w_m, w_l, w_r, w_f, w_g = packed
    h1 = _ln_modulate(x2d, mod[0], seq, shift_row=0, scale_row=1)
    for l in range(depth):
        b_m, b_l, b_r, b_f, b_g = (_pack_bias(b_in[l], seg, *grp) for grp in _PACK_GROUPS)
        y_m = _mlstm(h1, w_m, l, b_m, row(m_norm[l]), batch, seq)
        y_l = _rglru(h1, w_l, l, b_l, conv_w[l], row(conv_b[l]),
                     _block_diag(lru_wa[l]).astype(BF16), row(lru_ba[l]),
                     _block_diag(lru_wx[l]).astype(BF16), row(lru_bx[l]),
                     row(lru_lam[l]), batch, seq)
        y_r = _retention(h1, w_r, l, b_r, cos_t, sin_t, row(r_norm[l]), batch, seq)
        y_f = _fox(h1, w_f, l, b_f, batch, seq)
        x2d, h2 = _merge(h1, (y_m, y_l, y_r, y_f), x2d, mod[l], w_g, l, b_g,
                         w_br[l].astype(BF16), w_out[l].astype(BF16), row(b_out[l]),
                         row(ln1_g[l]), row(ln1_b[l]), seq, alpha)
        next_mod = mod[l + 1] if l + 1 < depth else None
        x2d, h1 = _mlp(h2, x2d, mod[l], next_mod, w_ff1[l].astype(BF16), row(b_ff1[l]),
                       w_ff2[l].astype(BF16), row(b_ff2[l]), row(ln2_g[l]), row(ln2_b[l]),
                       seq, alpha)
    return x2d.reshape(batch, seq, d)
```

```python
import functools
import math

import jax
import jax.numpy as jnp
from jax import lax
from jax.experimental import pallas as pl
from jax.experimental.pallas import tpu as pltpu

F32 = jnp.float32
BF16 = jnp.bfloat16

LN_EPS = 1e-5
HEADS = 4
DH = 128
BRANCH_W = HEADS * DH
LRU_BLOCKS = 8
LRU_C = 8.0
CONV_W = 4
ROPE_BASE = 10000.0
N_BRANCH = 4

LANES = 128
SUBLANES = 8
CHUNK = 256
ROW_TILE = 512
MERGE_SPLIT = 2
KV_GROUP = 2
VMEM_LIMIT_BYTES = 48 * 1024 * 1024

_SEGMENTS = (
    ("m_q", BRANCH_W), ("m_k", BRANCH_W), ("m_v", BRANCH_W), ("m_o", BRANCH_W),
    ("m_i", HEADS), ("m_f", HEADS),
    ("l_x", BRANCH_W), ("l_g", BRANCH_W),
    ("r_q", BRANCH_W), ("r_k", BRANCH_W), ("r_v", BRANCH_W), ("r_g", BRANCH_W),
    ("f_q", BRANCH_W), ("f_k", BRANCH_W), ("f_v", BRANCH_W), ("f_f", HEADS),
    ("gate", None),
)


def _segment_slices(d_model):
    out, acc = {}, 0
    for name, width in _SEGMENTS:
        width = N_BRANCH * d_model if width is None else width
        out[name] = (acc, acc + width)
        acc += width
    return out


def _dot(a, b):
    return jnp.dot(a, b, preferred_element_type=F32)


def _dot_f32(a, b):
    return jnp.dot(a, b, preferred_element_type=F32, precision=lax.Precision.HIGHEST)


def _layernorm(x):
    mu = jnp.mean(x, axis=-1, keepdims=True)
    xc = x - mu
    var = jnp.mean(xc * xc, axis=-1, keepdims=True)
    return xc * lax.rsqrt(var + LN_EPS)


def _log_sigmoid(x):
    return -(jnp.maximum(-x, 0.0) + jnp.log1p(jnp.exp(-jnp.abs(x))))


def _softplus(x):
    return jnp.maximum(x, 0.0) + jnp.log1p(jnp.exp(-jnp.abs(x)))


def _gelu_tanh(x):
    return x * (0.5 * (1.0 + jnp.tanh(math.sqrt(2.0 / math.pi) * (x + 0.044715 * (x * x * x)))))


def _causal_mask(n):
    row = lax.broadcasted_iota(jnp.int32, (n, n), 0)
    col = lax.broadcasted_iota(jnp.int32, (n, n), 1)
    return row >= col


def _prefix_sum_matrix(n):
    row = lax.broadcasted_iota(jnp.int32, (n, n), 0)
    col = lax.broadcasted_iota(jnp.int32, (n, n), 1)
    return jnp.where(row <= col, 1.0, 0.0).astype(F32)


def _ones_column_block(n):
    lane = lax.broadcasted_iota(jnp.int32, (n, LANES), 1)
    return jnp.where(lane == 0, 1.0, 0.0).astype(F32)


def _head_norm(y, gain):
    mu = jnp.mean(y, axis=-1, keepdims=True)
    yc = y - mu
    var = jnp.mean(yc * yc, axis=-1, keepdims=True)
    return yc * lax.rsqrt(var + LN_EPS) * gain


def _const_spec(shape):
    zeros = (0,) * len(shape)
    return pl.BlockSpec(shape, lambda *_: zeros, pipeline_mode=pl.Buffered(1))


def _layer_spec(shape, layer):
    idx = (layer,) + (0,) * len(shape)
    return pl.BlockSpec((None,) + tuple(shape), lambda *_: idx, pipeline_mode=pl.Buffered(1))


def _params(semantics):
    return pltpu.CompilerParams(
        dimension_semantics=semantics, vmem_limit_bytes=VMEM_LIMIT_BYTES)


def _mod_kernel(c_ref, w_ref, b_ref, o_ref):
    c = c_ref[...]
    cond = c * jax.nn.sigmoid(c)
    o_ref[...] = _dot_f32(cond, w_ref[...]) + b_ref[...]


def _modulation(c, w_ada, b_ada):
    depth, d, six_d = w_ada.shape
    batch = c.shape[0]
    n_tiles = six_d // d
    return pl.pallas_call(
        _mod_kernel,
        grid=(depth, n_tiles),
        in_specs=[
            pl.BlockSpec((batch, d), lambda l, j: (0, 0)),
            pl.BlockSpec((None, d, d), lambda l, j: (l, 0, j)),
            pl.BlockSpec((None, 1, d), lambda l, j: (l, 0, j)),
        ],
        out_specs=pl.BlockSpec((None, batch, d), lambda l, j: (l, 0, j)),
        out_shape=jax.ShapeDtypeStruct((depth, batch, six_d), F32),
        compiler_params=_params(("parallel", "parallel")),
        name="adaln_modulation",
    )(c, w_ada, b_ada.reshape(depth, 1, six_d))


def _rope_kernel(pos_ref, invf_ref, cos_ref, sin_ref):
    ang = invf_ref[...] * pos_ref[...].astype(F32)
    c = jnp.cos(ang)
    s = jnp.sin(ang)
    cos_ref[...] = jnp.concatenate([c, c], axis=0).T
    sin_ref[...] = jnp.concatenate([-s, s], axis=0).T


def _rope_tables(positions):
    batch, seq = positions.shape
    inv_freq = ROPE_BASE ** (-jnp.arange(0, DH, 2, dtype=F32) / DH)
    out = jax.ShapeDtypeStruct((batch, seq, DH), F32)
    return pl.pallas_call(
        _rope_kernel,
        grid=(batch,),
        in_specs=[
            pl.BlockSpec((None, 1, seq), lambda b: (b, 0, 0)),
            pl.BlockSpec((DH // 2, 1), lambda b: (0, 0)),
        ],
        out_specs=[pl.BlockSpec((None, seq, DH), lambda b: (b, 0, 0))] * 2,
        out_shape=[out, out],
        compiler_params=_params(("parallel",)),
        name="rope_tables",
    )(positions.reshape(batch, 1, seq), inv_freq.reshape(DH // 2, 1))


def _ln_modulated(x, mod_ref, shift_row, scale_row):
    scale = mod_ref[scale_row:scale_row + 1, :]
    shift = mod_ref[shift_row:shift_row + 1, :]
    return (_layernorm(x) * (1.0 + scale) + shift).astype(BF16)


def _ln_mod_kernel(x_ref, mod_ref, o_ref, *, shift_row, scale_row):
    o_ref[...] = _ln_modulated(x_ref[...], mod_ref, shift_row, scale_row)


def _ln_modulate(x2d, mod_l, seq, shift_row, scale_row):
    tokens, d = x2d.shape
    tm = 2 * ROW_TILE
    tiles_per_seq = seq // tm
    return pl.pallas_call(
        functools.partial(_ln_mod_kernel, shift_row=shift_row, scale_row=scale_row),
        grid=(tokens // tm,),
        in_specs=[
            pl.BlockSpec((tm, d), lambda i: (i, 0)),
            pl.BlockSpec((None, 6, d), lambda i: (i // tiles_per_seq, 0, 0)),
        ],
        out_specs=pl.BlockSpec((tm, d), lambda i: (i, 0)),
        out_shape=jax.ShapeDtypeStruct((tokens, d), BF16),
        compiler_params=_params(("parallel",)),
        name="ln_modulate",
    )(x2d, mod_l)


def _mlstm_kernel(h_ref, w_ref, b_ref, gain_ref, o_ref, state_ref, mc_ref, bc_ref, z_ref):
    seq = h_ref.shape[0]
    L = CHUNK
    nc = seq // L
    qkvo = HEADS * DH
    state_ref[...] = jnp.zeros_like(state_ref)
    mc_ref[...] = jnp.zeros_like(mc_ref)
    bc_ref[...] = jnp.zeros_like(bc_ref)

    def project(i):
        r0 = pl.multiple_of(i * L, L)
        z_ref[...] = _dot(h_ref[pl.ds(r0, L), :], w_ref[...]) + b_ref[...]

    def chunk(i, project_next):
        r0 = pl.multiple_of(i * L, L)
        z = z_ref[...]
        causal = _causal_mask(L)
        gates = z[:, 4 * qkvo:4 * qkvo + LANES].T[0:2 * HEADS, :]
        log_f = _log_sigmoid(gates)
        cum_f = _dot_f32(log_f, _prefix_sum_matrix(L))
        ones_blk = _ones_column_block(L)
        heads = []
        for hd in range(HEADS):
            lo = hd * DH
            heads.append((
                z[:, lo:lo + DH].astype(BF16),
                (z[:, qkvo + lo:qkvo + lo + DH] * DH ** -0.5).T,
                jnp.concatenate(
                    [z[:, 2 * qkvo + lo:2 * qkvo + lo + DH], ones_blk], axis=1).astype(BF16),
                jax.nn.sigmoid(z[:, 3 * qkvo + lo:3 * qkvo + lo + DH])))
        if project_next:
            project(i + 1)
        for hd, (q, k_t, v_aug, o_gate) in enumerate(heads):
            lo = hd * DH
            mc = mc_ref[hd][:, 0:1]
            bc = bc_ref[hd][:, 0:1]
            lf_row = log_f[HEADS + hd:HEADS + hd + 1, :]
            b_row = cum_f[HEADS + hd:HEADS + hd + 1, :] + bc
            c_row = gates[hd:hd + 1, :] - b_row
            d0 = jnp.where(causal, c_row, -jnp.inf)
            m_t = jnp.maximum(jnp.max(d0, axis=1, keepdims=True), mc)
            p = jnp.exp(d0 - m_t)
            b_col = jnp.sum(jnp.where(causal, lf_row, 0.0), axis=1, keepdims=True) + bc
            s = _dot(q, k_t.astype(BF16))
            intra = _dot((s * p).astype(BF16), v_aug)
            st = state_ref[hd]
            inter = _dot(q, st.astype(BF16))
            tot = jnp.exp(mc - m_t) * inter + intra
            den = tot[:, DH:DH + 1]
            hh = tot[:, :DH] / jnp.maximum(jnp.abs(den), jnp.exp(-(b_col + m_t)))
            y = _head_norm(o_gate * hh, gain_ref[:, lo:lo + DH])
            o_ref[pl.ds(r0, L), lo:lo + DH] = y.astype(BF16)
            mc_new = m_t[L - 1:L, :]
            k_w = (k_t * jnp.exp(c_row - mc_new)).astype(BF16)
            state_ref[hd] = jnp.exp(mc - mc_new) * st + _dot(k_w, v_aug)
            mc_ref[hd] = jnp.broadcast_to(mc_new, (1, LANES))
            bc_ref[hd] = jnp.broadcast_to(b_row[:, L - 1:L], (1, LANES))

    project(0)
    for i in range(nc):
        chunk(i, project_next=i + 1 < nc)


def _mlstm(h2d, w, layer, b, gain, batch, seq):
    d = h2d.shape[1]
    n = w.shape[2]
    return pl.pallas_call(
        _mlstm_kernel,
        grid=(batch,),
        in_specs=[
            pl.BlockSpec((seq, d), lambda i: (i, 0)),
            _layer_spec((d, n), layer),
            _const_spec((1, n)),
            _const_spec((1, BRANCH_W)),
        ],
        out_specs=pl.BlockSpec((seq, BRANCH_W), lambda i: (i, 0)),
        out_shape=jax.ShapeDtypeStruct((batch * seq, BRANCH_W), BF16),
        scratch_shapes=[
            pltpu.VMEM((HEADS, DH, 2 * DH), F32),
            pltpu.VMEM((HEADS, 1, LANES), F32),
            pltpu.VMEM((HEADS, 1, LANES), F32),
            pltpu.VMEM((CHUNK, n), F32),
        ],
        compiler_params=_params(("arbitrary",)),
        name="mlstm",
    )(h2d, w, b, gain)


def _rglru_kernel(h_ref, w_ref, b_ref, cw_ref, cb_ref, wa_ref, ba_ref, wx_ref, bx_ref,
                  lam_ref, o_ref, xpad_ref, hbuf_ref, hcar_ref):
    seq = h_ref.shape[0]
    L = CHUNK
    R = BRANCH_W
    pad = SUBLANES
    xpad_ref[0:pad, :] = jnp.zeros((pad, R), F32)
    hcar_ref[...] = jnp.zeros_like(hcar_ref)

    def step(i, carry):
        r0 = pl.multiple_of(i * L, L)
        z = _dot(h_ref[pl.ds(r0, L), :], w_ref[...]) + b_ref[...]
        xpad_ref[pad:pad + L, :] = z[:, :R]
        xp = xpad_ref[...]
        xc = cb_ref[...]
        for k in range(CONV_W):
            off = pad - (CONV_W - 1) + k
            xc = xc + cw_ref[k:k + 1, :] * xp[off:off + L, :]
        xpad_ref[0:pad, :] = xp[L:L + pad, :]
        xcb = xc.astype(BF16)
        r = jax.nn.sigmoid(_dot(xcb, wa_ref[...]) + ba_ref[...])
        ig = jax.nn.sigmoid(_dot(xcb, wx_ref[...]) + bx_ref[...])
        log_a = (-LRU_C) * r * _softplus(-lam_ref[...])
        a = jnp.exp(log_a)
        th = jnp.tanh(log_a)
        u = jnp.sqrt(-2.0 * th / (1.0 - th)) * (ig * xc)
        groups = L // SUBLANES
        a = a.reshape(groups, SUBLANES, R)
        u = u.reshape(groups, SUBLANES, R)
        sub = lax.broadcasted_iota(jnp.int32, (groups, SUBLANES, R), 1)
        for sft in (1, 2, 4):
            ok = sub >= sft
            u = jnp.where(ok, a * pltpu.roll(u, sft, 1) + u, u)
            a = jnp.where(ok, a * pltpu.roll(a, sft, 1), a)
        hprev = hcar_ref[...]
        for g in range(groups):
            rows = slice(g * SUBLANES, (g + 1) * SUBLANES)
            hg = a[g] * hprev + u[g]
            hbuf_ref[rows, :] = hg
            hprev = hg[SUBLANES - 1:SUBLANES, :]
        hcar_ref[...] = hprev
        o_ref[pl.ds(r0, L), :] = (hbuf_ref[...] * _gelu_tanh(z[:, R:2 * R])).astype(BF16)
        return carry

    for i in range(seq // L):
        step(i, 0)


def _rglru(h2d, w, layer, b, conv_w, conv_b, wa, ba, wx, bx, lam, batch, seq):
    d = h2d.shape[1]
    R = BRANCH_W
    return pl.pallas_call(
        _rglru_kernel,
        grid=(batch,),
        in_specs=[
            pl.BlockSpec((seq, d), lambda i: (i, 0)),
            _layer_spec((d, 2 * R), layer),
            _const_spec((1, 2 * R)),
            _const_spec((CONV_W, R)),
            _const_spec((1, R)),
            _const_spec((R, R)),
            _const_spec((1, R)),
            _const_spec((R, R)),
            _const_spec((1, R)),
            _const_spec((1, R)),
        ],
        out_specs=pl.BlockSpec((seq, R), lambda i: (i, 0)),
        out_shape=jax.ShapeDtypeStruct((batch * seq, R), BF16),
        scratch_shapes=[
            pltpu.VMEM((CHUNK + SUBLANES, R), F32),
            pltpu.VMEM((CHUNK, R), F32),
            pltpu.VMEM((1, R), F32),
        ],
        compiler_params=_params(("arbitrary",)),
        name="rglru",
    )(h2d, w, b, conv_w, conv_b, wa, ba, wx, bx, lam)


def _log_gamma(hd):
    return math.log1p(-(2.0 ** (-5.0 - hd)))


def _retention_kernel(h_ref, w_ref, b_ref, cos_ref, sin_ref, gain_ref, o_ref, state_ref,
                      decay_ref, z_ref):
    seq = h_ref.shape[0]
    L = CHUNK
    nc = seq // L
    W = HEADS * DH
    state_ref[...] = jnp.zeros_like(state_ref)
    row = lax.broadcasted_iota(jnp.int32, (L, L), 0)
    col = lax.broadcasted_iota(jnp.int32, (L, L), 1)
    rel = (row - col).astype(F32)
    for hd in range(HEADS):
        decay_ref[hd] = jnp.where(rel >= 0.0, jnp.exp(rel * _log_gamma(hd)), 0.0)

    def project(i):
        r0 = pl.multiple_of(i * L, L)
        z_ref[...] = _dot(h_ref[pl.ds(r0, L), :], w_ref[...]) + b_ref[...]

    def chunk(i, project_next):
        r0 = pl.multiple_of(i * L, L)
        z = z_ref[...]
        cosf = cos_ref[pl.ds(r0, L), :]
        sinf = sin_ref[pl.ds(r0, L), :]
        idx_col = lax.broadcasted_iota(jnp.int32, (L, 1), 0).astype(F32)
        idx_row = lax.broadcasted_iota(jnp.int32, (1, L), 1).astype(F32)
        heads = []
        for hd in range(HEADS):
            lo = hd * DH
            qf = z[:, lo:lo + DH]
            kf = z[:, W + lo:W + lo + DH]
            heads.append((
                (qf * cosf + pltpu.roll(qf, DH // 2, 1) * sinf).astype(BF16),
                ((kf * cosf + pltpu.roll(kf, DH // 2, 1) * sinf) * DH ** -0.5).T,
                z[:, 2 * W + lo:2 * W + lo + DH].astype(BF16),
                z[:, 3 * W + lo:3 * W + lo + DH]))
        if project_next:
            project(i + 1)
        for hd, (q, k_t, v, g) in enumerate(heads):
            lo = hd * DH
            lg = _log_gamma(hd)
            s = _dot(q, k_t.astype(BF16)) * decay_ref[hd]
            st = state_ref[hd]
            out = _dot(s.astype(BF16), v) + jnp.exp((idx_col + 1.0) * lg) * _dot(q, st.astype(BF16))
            k_w = (k_t * jnp.exp((L - 1.0 - idx_row) * lg)).astype(BF16)
            state_ref[hd] = math.exp(L * lg) * st + _dot(k_w, v)
            y = (g * jax.nn.sigmoid(g)) * _head_norm(out, gain_ref[:, lo:lo + DH])
            o_ref[pl.ds(r0, L), lo:lo + DH] = y.astype(BF16)

    project(0)
    for i in range(nc):
        chunk(i, project_next=i + 1 < nc)


def _retention(h2d, w, layer, b, cos_t, sin_t, gain, batch, seq):
    d = h2d.shape[1]
    n = w.shape[2]
    return pl.pallas_call(
        _retention_kernel,
        grid=(batch,),
        in_specs=[
            pl.BlockSpec((seq, d), lambda i: (i, 0)),
            _layer_spec((d, n), layer),
            _const_spec((1, n)),
            pl.BlockSpec((None, seq, DH), lambda i: (i, 0, 0)),
            pl.BlockSpec((None, seq, DH), lambda i: (i, 0, 0)),
            _const_spec((1, BRANCH_W)),
        ],
        out_specs=pl.BlockSpec((seq, BRANCH_W), lambda i: (i, 0)),
        out_shape=jax.ShapeDtypeStruct((batch * seq, BRANCH_W), BF16),
        scratch_shapes=[
            pltpu.VMEM((HEADS, DH, DH), F32),
            pltpu.VMEM((HEADS, CHUNK, CHUNK), F32),
            pltpu.VMEM((CHUNK, n), F32),
        ],
        compiler_params=_params(("arbitrary",)),
        name="retention",
    )(h2d, w, b, cos_t, sin_t, gain)


def _fox_kernel(h_ref, w_ref, b_ref, o_ref, kt_ref, v_ref, fc_ref, q_ref, acc_ref):
    seq = h_ref.shape[0]
    L = CHUNK
    W = HEADS * DH
    fc_ref[...] = jnp.zeros_like(fc_ref)

    def step(i, carry):
        r0 = pl.multiple_of(i * L, L)
        z = _dot(h_ref[pl.ds(r0, L), :], w_ref[...]) + b_ref[...]
        causal = _causal_mask(L)
        f_pre = z[:, 3 * W:3 * W + LANES].T[0:SUBLANES, :]
        cum_f = _dot_f32(_log_sigmoid(f_pre), _prefix_sum_matrix(L)) + fc_ref[:, 0:1]
        fc_ref[...] = jnp.broadcast_to(cum_f[:, L - 1:L], (SUBLANES, LANES))
        f_hi = cum_f.astype(BF16)
        rem = cum_f - f_hi.astype(F32)
        f_mid = rem.astype(BF16)
        f_lo = (rem - f_mid.astype(F32)).astype(BF16)
        sub = lax.broadcasted_iota(jnp.int32, (SUBLANES, L), 0)
        lane = lax.broadcasted_iota(jnp.int32, (L, LANES), 1)
        q_tail = jnp.where(lane < 3, 1.0, 0.0).astype(BF16)
        k_pad = jnp.zeros((DH - SUBLANES, L), F32)
        ones_blk = _ones_column_block(L)
        row_max = []
        for hd in range(HEADS):
            lo = hd * DH
            pieces = [jnp.broadcast_to(-f[hd:hd + 1, :].astype(F32), (SUBLANES, L))
                      for f in (f_hi, f_mid, f_lo)]
            bias_rows = jnp.where(sub == 0, pieces[0], jnp.where(
                sub == 1, pieces[1], jnp.where(sub == 2, pieces[2], 0.0)))
            k_t = jnp.concatenate(
                [z[:, W + lo:W + lo + DH].T, bias_rows, k_pad], axis=0).astype(BF16)
            v_aug = jnp.concatenate(
                [z[:, 2 * W + lo:2 * W + lo + DH], ones_blk], axis=1).astype(BF16)
            kt_ref[hd, i] = k_t
            v_ref[hd, i] = v_aug
            q = jnp.concatenate(
                [(z[:, lo:lo + DH] * DH ** -0.5).astype(BF16), q_tail], axis=1)
            q_ref[hd] = q
            s = jnp.where(causal, _dot(q, k_t), -jnp.inf)
            m = jnp.max(s, axis=1, keepdims=True)
            acc_ref[hd] = _dot(jnp.exp(s - m).astype(BF16), v_aug)
            row_max.append(m)

        def kv_steps(width):
            def body(t, ms, first):
                js = [first + t * width + u for u in range(width)]
                out = []
                for hd in range(HEADS):
                    ss = [_dot(q_ref[hd], kt_ref[hd, j]) for j in js]
                    s_max = functools.reduce(jnp.maximum, ss)
                    m_new = jnp.maximum(ms[hd], jnp.max(s_max, axis=1, keepdims=True))
                    acc = jnp.exp(ms[hd] - m_new) * acc_ref[hd]
                    for j, s in zip(js, ss):
                        acc = acc + _dot(jnp.exp(s - m_new).astype(BF16), v_ref[hd, j])
                    acc_ref[hd] = acc
                    out.append(m_new)
                return tuple(out)
            return body

        ms = tuple(row_max)
        for t in range(i):
            ms = kv_steps(1)(t, ms, 0)
        for hd in range(HEADS):
            acc = acc_ref[hd]
            o_ref[pl.ds(r0, L), hd * DH:(hd + 1) * DH] = (
                acc[:, :DH] / acc[:, DH:DH + 1]).astype(BF16)
        return carry

    for ii in range(seq // L):
        step(ii, 0)


def _fox(h2d, w, layer, b, batch, seq):
    d = h2d.shape[1]
    n = w.shape[2]
    nc = seq // CHUNK
    return pl.pallas_call(
        _fox_kernel,
        grid=(batch,),
        in_specs=[
            pl.BlockSpec((seq, d), lambda i: (i, 0)),
            _layer_spec((d, n), layer),
            _const_spec((1, n)),
        ],
        out_specs=pl.BlockSpec((seq, BRANCH_W), lambda i: (i, 0)),
        out_shape=jax.ShapeDtypeStruct((batch * seq, BRANCH_W), BF16),
        scratch_shapes=[
            pltpu.VMEM((HEADS, nc, 2 * DH, CHUNK), BF16),
            pltpu.VMEM((HEADS, nc, CHUNK, 2 * DH), BF16),
            pltpu.VMEM((SUBLANES, LANES), F32),
            pltpu.VMEM((HEADS, CHUNK, 2 * DH), BF16),
            pltpu.VMEM((HEADS, CHUNK, 2 * DH), F32),
        ],
        compiler_params=_params(("arbitrary",)),
        name="forgetting_attention",
    )(h2d, w, b)


def _merge_kernel(h_ref, ym_ref, yl_ref, yr_ref, yf_ref, x_ref, mod_ref, wg_ref, bg_ref,
                  wbr_ref, wo_ref, bo_ref, lng_ref, lnb_ref, o_ref, hn_ref, *, alpha):
    tm, d = x_ref.shape
    for r0 in range(0, tm, tm // MERGE_SPLIT):
        rows = slice(r0, r0 + tm // MERGE_SPLIT)
        h = h_ref[rows, :]
        merged = None
        for n, y_ref in enumerate((ym_ref, yl_ref, yr_ref, yf_ref)):
            gate = jax.nn.sigmoid(
                _dot(h, wg_ref[:, n * d:(n + 1) * d]) + bg_ref[:, n * d:(n + 1) * d])
            term = gate * _dot(y_ref[rows, :], wbr_ref[n])
            merged = term if merged is None else merged + term
        y = _dot(merged.astype(BF16), wo_ref[...]) + bo_ref[...]
        res = alpha * x_ref[rows, :] + (1.0 + mod_ref[2:3, :]) * y
        x_new = _layernorm(res) * lng_ref[...] + lnb_ref[...]
        o_ref[rows, :] = x_new
        hn_ref[rows, :] = _ln_modulated(x_new, mod_ref, shift_row=3, scale_row=4)


def _merge(h2d, ys, x2d, mod_l, wg, layer, bg, wbr, wo, bo, lng, lnb, seq, alpha):
    tokens, d = x2d.shape
    tm = ROW_TILE
    tiles_per_seq = seq // tm
    row = lambda i: (i, 0)
    return pl.pallas_call(
        functools.partial(_merge_kernel, alpha=alpha),
        grid=(tokens // tm,),
        in_specs=[
            pl.BlockSpec((tm, d), row),
            pl.BlockSpec((tm, BRANCH_W), row),
            pl.BlockSpec((tm, BRANCH_W), row),
            pl.BlockSpec((tm, BRANCH_W), row),
            pl.BlockSpec((tm, BRANCH_W), row),
            pl.BlockSpec((tm, d), row),
            pl.BlockSpec((None, 6, d), lambda i: (i // tiles_per_seq, 0, 0)),
            _layer_spec((d, N_BRANCH * d), layer),
            _const_spec((1, N_BRANCH * d)),
            _const_spec((N_BRANCH, BRANCH_W, d)),
            _const_spec((d, d)),
            _const_spec((1, d)),
            _const_spec((1, d)),
            _const_spec((1, d)),
        ],
        out_specs=[pl.BlockSpec((tm, d), row), pl.BlockSpec((tm, d), row)],
        out_shape=[jax.ShapeDtypeStruct((tokens, d), F32),
                   jax.ShapeDtypeStruct((tokens, d), BF16)],
        compiler_params=_params(("parallel",)),
        name="merge_out_proj",
    )(h2d, *ys, x2d, mod_l, wg, bg, wbr, wo, bo, lng, lnb)


def _mlp_kernel(h_ref, x_ref, mod_ref, w1_ref, b1_ref, w2_ref, b2_ref, lng_ref, lnb_ref,
                *rest, alpha, ff_tile, emit_next):
    d_ff = w1_ref.shape[1]
    tm = x_ref.shape[0]
    if emit_next:
        next_mod_ref, o_ref, hn_ref = rest
    else:
        (o_ref,) = rest
    for r0 in range(0, tm, tm // MERGE_SPLIT):
        rows = slice(r0, r0 + tm // MERGE_SPLIT)
        h = h_ref[rows, :]
        acc = None
        for c in range(d_ff // ff_tile):
            cols = slice(c * ff_tile, (c + 1) * ff_tile)
            a = jnp.maximum(_dot(h, w1_ref[:, cols]) + b1_ref[:, cols], 0.0)
            part = _dot((a * a).astype(BF16), w2_ref[cols, :])
            acc = part if acc is None else acc + part
        y = acc + b2_ref[...]
        res = alpha * x_ref[rows, :] + (1.0 + mod_ref[5:6, :]) * y
        x_new = _layernorm(res) * lng_ref[...] + lnb_ref[...]
        o_ref[rows, :] = x_new
        if emit_next:
            hn_ref[rows, :] = _ln_modulated(x_new, next_mod_ref, shift_row=0, scale_row=1)


def _mlp(h2d, x2d, mod_l, next_mod, w1, b1, w2, b2, lng, lnb, seq, alpha):
    tokens, d = x2d.shape
    d_ff = w1.shape[1]
    tm = ROW_TILE
    tiles_per_seq = seq // tm
    row = lambda i: (i, 0)
    mod_spec = pl.BlockSpec((None, 6, d), lambda i: (i // tiles_per_seq, 0, 0))
    emit_next = next_mod is not None
    in_specs = [
        pl.BlockSpec((tm, d), row),
        pl.BlockSpec((tm, d), row),
        mod_spec,
        _const_spec((d, d_ff)),
        _const_spec((1, d_ff)),
        _const_spec((d_ff, d)),
        _const_spec((1, d)),
        _const_spec((1, d)),
        _const_spec((1, d)),
    ]
    args = [h2d, x2d, mod_l, w1, b1, w2, b2, lng, lnb]
    out_specs = [pl.BlockSpec((tm, d), row)]
    out_shape = [jax.ShapeDtypeStruct((tokens, d), F32)]
    if emit_next:
        in_specs.append(mod_spec)
        args.append(next_mod)
        out_specs.append(pl.BlockSpec((tm, d), row))
        out_shape.append(jax.ShapeDtypeStruct((tokens, d), BF16))
    outs = pl.pallas_call(
        functools.partial(_mlp_kernel, alpha=alpha, ff_tile=d, emit_next=emit_next),
        grid=(tokens // tm,),
        in_specs=in_specs,
        out_specs=out_specs,
        out_shape=out_shape,
        compiler_params=_params(("parallel",)),
        name="sq_relu_mlp",
    )(*args)
    return (outs[0], outs[1]) if emit_next else (outs[0], None)


_PACK_GROUPS = (
    (("m_q", "m_k", "m_v", "m_o"), ("m_i", "m_f")),
    (("l_x", "l_g"), ()),
    (("r_q", "r_k", "r_v", "r_g"), ()),
    (("f_q", "f_k", "f_v"), ("f_f",)),
    (("gate",), ()),
)
def _pack_kernel(w_ref, o_ref, *, n_blocks, last_valid):
    depth, d = w_ref.shape[1], w_ref.shape[2]
    lane = lax.broadcasted_iota(jnp.int32, (d, LANES), 1)
    n_valid = jnp.where(pl.program_id(0) == n_blocks - 1, last_valid, LANES)
    for l in range(depth):
        o_ref[l] = jnp.where(lane < n_valid, w_ref[:, l, :].T, 0.0).astype(BF16)


def _pack_group(w_cols, seg, names, gate_names):
    n_in, depth, d = w_cols.shape
    start = seg[names[0]][0]
    main = seg[names[-1]][1] - start
    assert main % LANES == 0
    gate_w = sum(seg[n][1] - seg[n][0] for n in gate_names)
    assert not gate_names or seg[gate_names[0]][0] == start + main
    n_blocks = main // LANES + (1 if gate_names else 0)
    return pl.pallas_call(
        functools.partial(_pack_kernel, n_blocks=n_blocks,
                          last_valid=gate_w if gate_names else LANES),
        grid=(n_blocks,),
        in_specs=[pl.BlockSpec((pl.Element(LANES), pl.Element(depth), pl.Element(d)),
                               lambda j: (start + j * LANES, 0, 0))],
        out_specs=pl.BlockSpec((depth, d, LANES), lambda j: (0, 0, j)),
        out_shape=jax.ShapeDtypeStruct((depth, d, n_blocks * LANES), BF16),
        compiler_params=_params(("parallel",)),
        name="pack_in_proj",
    )(w_cols)


def _pack_weights(w_in, seg):
    w_cols = jnp.transpose(w_in, (2, 0, 1))
    return [_pack_group(w_cols, seg, names, gates) for names, gates in _PACK_GROUPS]


def _pack_bias(b_in_l, seg, names, gate_names):
    bs = [b_in_l[seg[n][0]:seg[n][1]] for n in names]
    if gate_names:
        gb = jnp.concatenate([b_in_l[seg[n][0]:seg[n][1]] for n in gate_names])
        bs.append(jnp.pad(gb, (0, LANES - gb.shape[0])))
    return jnp.concatenate(bs)[None, :]


def _block_diag(w):
    nb, bd, _ = w.shape
    eye = jnp.eye(nb, dtype=w.dtype)
    return (eye[:, None, :, None] * w[:, :, None, :]).reshape(nb * bd, nb * bd)


def kernel(x, c, positions, w_ada, b_ada, w_in, b_in, m_norm, conv_w, conv_b, lru_wa, lru_ba,
           lru_wx, lru_bx, lru_lam, r_norm, w_br, w_out, b_out, ln1_g, ln1_b, w_ff1, b_ff1,
           w_ff2, b_ff2, ln2_g, ln2_b):
    batch, seq, d = x.shape
    depth = w_ada.shape[0]
    assert seq % CHUNK == 0 and seq % (2 * ROW_TILE) == 0 and d % LANES == 0
    alpha = (2 * depth) ** 0.25
    seg = _segment_slices(d)

    mod = _modulation(c, w_ada, b_ada).reshape(depth, batch, 6, d)
    cos_t, sin_t = _rope_tables(positions)
    x2d = x.reshape(batch * seq, d)
    packed = _pack_weights(w_in, seg)
    row = lambda v: v[None, :]
    w_m, w_l, w_r, w_f, w_g = packed
    h1 = _ln_modulate(x2d, mod[0], seq, shift_row=0, scale_row=1)
    for l in range(depth):
        b_m, b_l, b_r, b_f, b_g = (_pack_bias(b_in[l], seg, *grp) for grp in _PACK_GROUPS)
        y_m = _mlstm(h1, w_m, l, b_m, row(m_norm[l]), batch, seq)
        y_l = _rglru(h1, w_l, l, b_l, conv_w[l], row(conv_b[l]),
                     _block_diag(lru_wa[l]).astype(BF16), row(lru_ba[l]),
                     _block_diag(lru_wx[l]).astype(BF16), row(lru_bx[l]),
                     row(lru_lam[l]), batch, seq)
        y_r = _retention(h1, w_r, l, b_r, cos_t, sin_t, row(r_norm[l]), batch, seq)
        y_f = _fox(h1, w_f, l, b_f, batch, seq)
        x2d, h2 = _merge(h1, (y_m, y_l, y_r, y_f), x2d, mod[l], w_g, l, b_g,
                         w_br[l].astype(BF16), w_out[l].astype(BF16), row(b_out[l]),
                         row(ln1_g[l]), row(ln1_b[l]), seq, alpha)
        next_mod = mod[l + 1] if l + 1 < depth else None
        x2d, h1 = _mlp(h2, x2d, mod[l], next_mod, w_ff1[l].astype(BF16), row(b_ff1[l]),
                       w_ff2[l].astype(BF16), row(b_ff2[l]), row(ln2_g[l]), row(ln2_b[l]),
                       seq, alpha)
    return x2d.reshape(batch, seq, d)
```

```python
import functools
import math

import jax
import jax.numpy as jnp
from jax import lax
from jax.experimental import pallas as pl
from jax.experimental.pallas import tpu as pltpu

F32 = jnp.float32
BF16 = jnp.bfloat16

LN_EPS = 1e-5
HEADS = 4
DH = 128
BRANCH_W = HEADS * DH
LRU_BLOCKS = 8
LRU_C = 8.0
CONV_W = 4
ROPE_BASE = 10000.0
N_BRANCH = 4

LANES = 128
SUBLANES = 8
CHUNK = 256
ROW_TILE = 1024
MERGE_SPLIT = 4
VMEM_LIMIT_BYTES = 48 * 1024 * 1024
VMEM_LIMIT_BYTES_WIDE = 58 * 1024 * 1024

_SEGMENTS = (
    ("m_q", BRANCH_W), ("m_k", BRANCH_W), ("m_v", BRANCH_W), ("m_o", BRANCH_W),
    ("m_i", HEADS), ("m_f", HEADS),
    ("l_x", BRANCH_W), ("l_g", BRANCH_W),
    ("r_q", BRANCH_W), ("r_k", BRANCH_W), ("r_v", BRANCH_W), ("r_g", BRANCH_W),
    ("f_q", BRANCH_W), ("f_k", BRANCH_W), ("f_v", BRANCH_W), ("f_f", HEADS),
    ("gate", None),
)


def _segment_slices(d_model):
    out, acc = {}, 0
    for name, width in _SEGMENTS:
        width = N_BRANCH * d_model if width is None else width
        out[name] = (acc, acc + width)
        acc += width
    return out


def _dot(a, b):
    return jnp.dot(a, b, preferred_element_type=F32)


def _dot_f32(a, b):
    return jnp.dot(a, b, preferred_element_type=F32, precision=lax.Precision.HIGHEST)


def _layernorm(x):
    mu = jnp.mean(x, axis=-1, keepdims=True)
    xc = x - mu
    var = jnp.mean(xc * xc, axis=-1, keepdims=True)
    return xc * lax.rsqrt(var + LN_EPS)


def _log_sigmoid(x):
    return -(jnp.maximum(-x, 0.0) + jnp.log1p(jnp.exp(-jnp.abs(x))))


def _softplus(x):
    return jnp.maximum(x, 0.0) + jnp.log1p(jnp.exp(-jnp.abs(x)))


def _gelu_tanh(x):
    return x * (0.5 * (1.0 + jnp.tanh(math.sqrt(2.0 / math.pi) * (x + 0.044715 * (x * x * x)))))


def _causal_mask(n):
    row = lax.broadcasted_iota(jnp.int32, (n, n), 0)
    col = lax.broadcasted_iota(jnp.int32, (n, n), 1)
    return row >= col


def _prefix_sum_matrix(n, dtype):
    row = lax.broadcasted_iota(jnp.int32, (n, n), 0)
    col = lax.broadcasted_iota(jnp.int32, (n, n), 1)
    return jnp.where(row <= col, 1.0, 0.0).astype(dtype)


def _bf16_pieces(x):
    hi = x.astype(BF16).astype(F32)
    mid = (x - hi).astype(BF16).astype(F32)
    lo = (x - hi - mid).astype(BF16).astype(F32)
    return hi, mid, lo


def _prefix_sum_lanes(x, tri):
    rows = x.shape[0]
    stacked = jnp.concatenate(_bf16_pieces(x) + (jnp.zeros_like(x),), axis=0).astype(BF16)
    y = _dot(stacked, tri)
    return y[0:rows] + y[rows:2 * rows] + y[2 * rows:3 * rows]


def _ones_column_block(n):
    lane = lax.broadcasted_iota(jnp.int32, (n, LANES), 1)
    return jnp.where(lane == 0, 1.0, 0.0).astype(F32)


def _head_norm(y, gain):
    mu = jnp.mean(y, axis=-1, keepdims=True)
    yc = y - mu
    var = jnp.mean(yc * yc, axis=-1, keepdims=True)
    return yc * lax.rsqrt(var + LN_EPS) * gain


def _const_spec(shape):
    zeros = (0,) * len(shape)
    return pl.BlockSpec(shape, lambda *_: zeros, pipeline_mode=pl.Buffered(1))


def _layer_spec(shape, layer):
    idx = (layer,) + (0,) * len(shape)
    return pl.BlockSpec((None,) + tuple(shape), lambda *_: idx, pipeline_mode=pl.Buffered(1))


def _params(semantics, vmem_limit=VMEM_LIMIT_BYTES):
    return pltpu.CompilerParams(dimension_semantics=semantics, vmem_limit_bytes=vmem_limit)


def _mod_kernel(c_ref, w_ref, b_ref, o_ref):
    c = c_ref[...]
    cond = c * jax.nn.sigmoid(c)
    o_ref[...] = _dot_f32(cond, w_ref[...]) + b_ref[...]


def _modulation(c, w_ada, b_ada):
    depth, d, six_d = w_ada.shape
    batch = c.shape[0]
    n_tiles = six_d // d
    return pl.pallas_call(
        _mod_kernel,
        grid=(depth, n_tiles),
        in_specs=[
            pl.BlockSpec((batch, d), lambda l, j: (0, 0)),
            pl.BlockSpec((None, d, d), lambda l, j: (l, 0, j)),
            pl.BlockSpec((None, 1, d), lambda l, j: (l, 0, j)),
        ],
        out_specs=pl.BlockSpec((None, batch, d), lambda l, j: (l, 0, j)),
        out_shape=jax.ShapeDtypeStruct((depth, batch, six_d), F32),
        compiler_params=_params(("parallel", "parallel")),
        name="adaln_modulation",
    )(c, w_ada, b_ada.reshape(depth, 1, six_d))


def _rope_kernel(pos_ref, invf_ref, cos_ref, sin_ref):
    ang = invf_ref[...] * pos_ref[...].astype(F32)
    c = jnp.cos(ang)
    s = jnp.sin(ang)
    cos_ref[...] = jnp.concatenate([c, c], axis=0).T
    sin_ref[...] = jnp.concatenate([-s, s], axis=0).T


def _rope_tables(positions):
    batch, seq = positions.shape
    inv_freq = ROPE_BASE ** (-jnp.arange(0, DH, 2, dtype=F32) / DH)
    out = jax.ShapeDtypeStruct((batch, seq, DH), F32)
    return pl.pallas_call(
        _rope_kernel,
        grid=(batch,),
        in_specs=[
            pl.BlockSpec((None, 1, seq), lambda b: (b, 0, 0)),
            pl.BlockSpec((DH // 2, 1), lambda b: (0, 0)),
        ],
        out_specs=[pl.BlockSpec((None, seq, DH), lambda b: (b, 0, 0))] * 2,
        out_shape=[out, out],
        compiler_params=_params(("parallel",)),
        name="rope_tables",
    )(positions.reshape(batch, 1, seq), inv_freq.reshape(DH // 2, 1))


def _ln_modulated(x, mod_ref, shift_row, scale_row):
    scale = mod_ref[scale_row:scale_row + 1, :]
    shift = mod_ref[shift_row:shift_row + 1, :]
    return (_layernorm(x) * (1.0 + scale) + shift).astype(BF16)


def _ln_mod_kernel(x_ref, mod_ref, o_ref, *, shift_row, scale_row):
    o_ref[...] = _ln_modulated(x_ref[...], mod_ref, shift_row, scale_row)


def _ln_modulate(x2d, mod_l, seq, shift_row, scale_row):
    tokens, d = x2d.shape
    tm = ROW_TILE
    tiles_per_seq = seq // tm
    return pl.pallas_call(
        functools.partial(_ln_mod_kernel, shift_row=shift_row, scale_row=scale_row),
        grid=(tokens // tm,),
        in_specs=[
            pl.BlockSpec((tm, d), lambda i: (i, 0)),
            pl.BlockSpec((None, 6, d), lambda i: (i // tiles_per_seq, 0, 0)),
        ],
        out_specs=pl.BlockSpec((tm, d), lambda i: (i, 0)),
        out_shape=jax.ShapeDtypeStruct((tokens, d), BF16),
        compiler_params=_params(("parallel",)),
        name="ln_modulate",
    )(x2d, mod_l)


def _mlstm_kernel(h_ref, w_ref, b_ref, gain_ref, o_ref, state_ref, mc_ref, bc_ref, z_ref):
    seq = h_ref.shape[0]
    L = CHUNK
    nc = seq // L
    qkvo = HEADS * DH
    state_ref[...] = jnp.zeros_like(state_ref)
    mc_ref[...] = jnp.zeros_like(mc_ref)
    bc_ref[...] = jnp.zeros_like(bc_ref)
    tri = _prefix_sum_matrix(L, F32)

    def project(i):
        r0 = pl.multiple_of(i * L, L)
        z_ref[...] = _dot(h_ref[pl.ds(r0, L), :], w_ref[...]) + b_ref[...]

    def chunk(i, project_next):
        r0 = pl.multiple_of(i * L, L)
        z = z_ref[...]
        causal = _causal_mask(L)
        gates = z[:, 4 * qkvo:4 * qkvo + LANES].T[0:2 * HEADS, :]
        log_f = _log_sigmoid(gates)
        cum_f = _dot_f32(log_f, tri)
        ones_blk = _ones_column_block(L)
        heads = []
        for hd in range(HEADS):
            lo = hd * DH
            heads.append((
                z[:, lo:lo + DH].astype(BF16),
                (z[:, qkvo + lo:qkvo + lo + DH] * DH ** -0.5).T,
                jnp.concatenate(
                    [z[:, 2 * qkvo + lo:2 * qkvo + lo + DH], ones_blk], axis=1).astype(BF16),
                jax.nn.sigmoid(z[:, 3 * qkvo + lo:3 * qkvo + lo + DH])))
        if project_next:
            project(i + 1)
        for hd, (q, k_t, v_aug, o_gate) in enumerate(heads):
            lo = hd * DH
            mc = mc_ref[hd][:, 0:1]
            bc = bc_ref[hd][:, 0:1]
            lf_row = log_f[HEADS + hd:HEADS + hd + 1, :]
            b_row = cum_f[HEADS + hd:HEADS + hd + 1, :] + bc
            c_row = gates[hd:hd + 1, :] - b_row
            d0 = jnp.where(causal, c_row, -jnp.inf)
            m_t = jnp.maximum(jnp.max(d0, axis=1, keepdims=True), mc)
            p = jnp.exp(d0 - m_t)
            b_col = jnp.sum(jnp.where(causal, lf_row, 0.0), axis=1, keepdims=True) + bc
            s = _dot(q, k_t.astype(BF16))
            intra = _dot((s * p).astype(BF16), v_aug)
            st = state_ref[hd]
            inter = _dot(q, st.astype(BF16))
            tot = jnp.exp(mc - m_t) * inter + intra
            den = tot[:, DH:DH + 1]
            hh = tot[:, :DH] / jnp.maximum(jnp.abs(den), jnp.exp(-(b_col + m_t)))
            y = _head_norm(o_gate * hh, gain_ref[:, lo:lo + DH])
            o_ref[pl.ds(r0, L), lo:lo + DH] = y.astype(BF16)
            mc_new = m_t[L - 1:L, :]
            k_w = (k_t * jnp.exp(c_row - mc_new)).astype(BF16)
            state_ref[hd] = jnp.exp(mc - mc_new) * st + _dot(k_w, v_aug)
            mc_ref[hd] = jnp.broadcast_to(mc_new, (1, LANES))
            bc_ref[hd] = jnp.broadcast_to(b_row[:, L - 1:L], (1, LANES))

    project(0)
    for i in range(nc):
        chunk(i, project_next=i + 1 < nc)


def _mlstm(h2d, w, layer, b, gain, batch, seq):
    d = h2d.shape[1]
    n = w.shape[2]
    return pl.pallas_call(
        _mlstm_kernel,
        grid=(batch,),
        in_specs=[
            pl.BlockSpec((seq, d), lambda i: (i, 0)),
            _layer_spec((d, n), layer),
            _const_spec((1, n)),
            _const_spec((1, BRANCH_W)),
        ],
        out_specs=pl.BlockSpec((seq, BRANCH_W), lambda i: (i, 0)),
        out_shape=jax.ShapeDtypeStruct((batch * seq, BRANCH_W), BF16),
        scratch_shapes=[
            pltpu.VMEM((HEADS, DH, 2 * DH), F32),
            pltpu.VMEM((HEADS, 1, LANES), F32),
            pltpu.VMEM((HEADS, 1, LANES), F32),
            pltpu.VMEM((CHUNK, n), F32),
        ],
        compiler_params=_params(("arbitrary",)),
        name="mlstm",
    )(h2d, w, b, gain)


def _rglru_kernel(h_ref, w_ref, b_ref, cw_ref, cb_ref, wa_ref, ba_ref, wx_ref, bx_ref,
                  lam_ref, o_ref, xpad_ref, hbuf_ref, hcar_ref):
    seq = h_ref.shape[0]
    L = CHUNK
    R = BRANCH_W
    pad = SUBLANES
    xpad_ref[0:pad, :] = jnp.zeros((pad, R), F32)
    hcar_ref[...] = jnp.zeros_like(hcar_ref)

    groups = L // SUBLANES

    def step(i, carry):
        r0 = pl.multiple_of(i * L, L)
        z = _dot(h_ref[pl.ds(r0, L), :], w_ref[...]) + b_ref[...]
        xpad_ref[pad:pad + L, :] = z[:, :R]
        xp = xpad_ref[...].reshape(groups + 1, SUBLANES, R)
        xpad_ref[0:pad, :] = xp[groups]
        sub = lax.broadcasted_iota(jnp.int32, (groups, SUBLANES, R), 1)
        xc = cb_ref[...] + cw_ref[CONV_W - 1:CONV_W, :] * xp[1:]
        for k in range(1, CONV_W):
            rot = pltpu.roll(xp, k, 1)
            xc = xc + cw_ref[CONV_W - 1 - k:CONV_W - k, :] * jnp.where(sub >= k, rot[1:], rot[:-1])
        xc = xc.reshape(L, R)
        xcb = xc.astype(BF16)
        r = jax.nn.sigmoid(_dot(xcb, wa_ref[...]) + ba_ref[...])
        ig = jax.nn.sigmoid(_dot(xcb, wx_ref[...]) + bx_ref[...])
        log_a = (-LRU_C) * r * _softplus(-lam_ref[...])
        a = jnp.exp(log_a)
        th = jnp.tanh(log_a)
        u = jnp.exp(0.5 * jnp.log(-2.0 * th / (1.0 - th))) * (ig * xc)
        a = a.reshape(groups, SUBLANES, R)
        u = u.reshape(groups, SUBLANES, R)
        for sft in (1, 2, 4):
            ok = sub >= sft
            u = jnp.where(ok, a * pltpu.roll(u, sft, 1) + u, u)
            a = jnp.where(ok, a * pltpu.roll(a, sft, 1), a)
        hprev = hcar_ref[...]
        for g in range(groups):
            rows = slice(g * SUBLANES, (g + 1) * SUBLANES)
            hg = a[g] * hprev + u[g]
            hbuf_ref[rows, :] = hg
            hprev = hg[SUBLANES - 1:SUBLANES, :]
        hcar_ref[...] = hprev
        o_ref[pl.ds(r0, L), :] = (hbuf_ref[...] * _gelu_tanh(z[:, R:2 * R])).astype(BF16)
        return carry

    lax.fori_loop(0, seq // L, step, 0)


def _rglru(h2d, w, layer, b, conv_w, conv_b, wa, ba, wx, bx, lam, batch, seq):
    d = h2d.shape[1]
    R = BRANCH_W
    return pl.pallas_call(
        _rglru_kernel,
        grid=(batch,),
        in_specs=[
            pl.BlockSpec((seq, d), lambda i: (i, 0)),
            _layer_spec((d, 2 * R), layer),
            _const_spec((1, 2 * R)),
            _const_spec((CONV_W, R)),
            _const_spec((1, R)),
            _const_spec((R, R)),
            _const_spec((1, R)),
            _const_spec((R, R)),
            _const_spec((1, R)),
            _const_spec((1, R)),
        ],
        out_specs=pl.BlockSpec((seq, R), lambda i: (i, 0)),
        out_shape=jax.ShapeDtypeStruct((batch * seq, R), BF16),
        scratch_shapes=[
            pltpu.VMEM((CHUNK + SUBLANES, R), F32),
            pltpu.VMEM((CHUNK, R), F32),
            pltpu.VMEM((1, R), F32),
        ],
        compiler_params=_params(("arbitrary",)),
        name="rglru",
    )(h2d, w, b, conv_w, conv_b, wa, ba, wx, bx, lam)


def _log_gamma(hd):
    return math.log1p(-(2.0 ** (-5.0 - hd)))


def _retention_kernel(h_ref, w_ref, b_ref, cos_ref, sin_ref, gain_ref, o_ref, state_ref,
                      decay_ref, z_ref):
    seq = h_ref.shape[0]
    L = CHUNK
    nc = seq // L
    W = HEADS * DH
    state_ref[...] = jnp.zeros_like(state_ref)
    row = lax.broadcasted_iota(jnp.int32, (L, L), 0)
    col = lax.broadcasted_iota(jnp.int32, (L, L), 1)
    rel = (row - col).astype(F32)
    for hd in range(HEADS):
        decay_ref[hd] = jnp.where(rel >= 0.0, jnp.exp(rel * _log_gamma(hd)), 0.0)

    def project(i):
        r0 = pl.multiple_of(i * L, L)
        z_ref[...] = _dot(h_ref[pl.ds(r0, L), :], w_ref[...]) + b_ref[...]

    def chunk(i, project_next):
        r0 = pl.multiple_of(i * L, L)
        z = z_ref[...]
        cosf = cos_ref[pl.ds(r0, L), :]
        sinf = sin_ref[pl.ds(r0, L), :]
        idx_col = lax.broadcasted_iota(jnp.int32, (L, 1), 0).astype(F32)
        idx_row = lax.broadcasted_iota(jnp.int32, (1, L), 1).astype(F32)
        heads = []
        for hd in range(HEADS):
            lo = hd * DH
            qf = z[:, lo:lo + DH]
            kf = z[:, W + lo:W + lo + DH]
            heads.append((
                (qf * cosf + pltpu.roll(qf, DH // 2, 1) * sinf).astype(BF16),
                ((kf * cosf + pltpu.roll(kf, DH // 2, 1) * sinf) * DH ** -0.5).T,
                z[:, 2 * W + lo:2 * W + lo + DH].astype(BF16),
                z[:, 3 * W + lo:3 * W + lo + DH]))
        if project_next:
            project(i + 1)
        for hd, (q, k_t, v, g) in enumerate(heads):
            lo = hd * DH
            lg = _log_gamma(hd)
            s = _dot(q, k_t.astype(BF16)) * decay_ref[hd]
            st = state_ref[hd]
            out = _dot(s.astype(BF16), v) + jnp.exp((idx_col + 1.0) * lg) * _dot(q, st.astype(BF16))
            k_w = (k_t * jnp.exp((L - 1.0 - idx_row) * lg)).astype(BF16)
            state_ref[hd] = math.exp(L * lg) * st + _dot(k_w, v)
            y = (g * jax.nn.sigmoid(g)) * _head_norm(out, gain_ref[:, lo:lo + DH])
            o_ref[pl.ds(r0, L), lo:lo + DH] = y.astype(BF16)

    project(0)
    for i in range(nc):
        chunk(i, project_next=i + 1 < nc)


def _retention(h2d, w, layer, b, cos_t, sin_t, gain, batch, seq):
    d = h2d.shape[1]
    n = w.shape[2]
    return pl.pallas_call(
        _retention_kernel,
        grid=(batch,),
        in_specs=[
            pl.BlockSpec((seq, d), lambda i: (i, 0)),
            _layer_spec((d, n), layer),
            _const_spec((1, n)),
            pl.BlockSpec((None, seq, DH), lambda i: (i, 0, 0)),
            pl.BlockSpec((None, seq, DH), lambda i: (i, 0, 0)),
            _const_spec((1, BRANCH_W)),
        ],
        out_specs=pl.BlockSpec((seq, BRANCH_W), lambda i: (i, 0)),
        out_shape=jax.ShapeDtypeStruct((batch * seq, BRANCH_W), BF16),
        scratch_shapes=[
            pltpu.VMEM((HEADS, DH, DH), F32),
            pltpu.VMEM((HEADS, CHUNK, CHUNK), F32),
            pltpu.VMEM((CHUNK, n), F32),
        ],
        compiler_params=_params(("arbitrary",)),
        name="retention",
    )(h2d, w, b, cos_t, sin_t, gain)


def _fox_kernel(h_ref, w_ref, b_ref, o_ref, kt_ref, v_ref, fc_ref, q_ref, acc_ref):
    seq = h_ref.shape[0]
    L = CHUNK
    W = HEADS * DH
    fc_ref[...] = jnp.zeros_like(fc_ref)
    tri = _prefix_sum_matrix(L, BF16)
    causal = _causal_mask(L)
    sub = lax.broadcasted_iota(jnp.int32, (SUBLANES, L), 0)
    lane = lax.broadcasted_iota(jnp.int32, (L, LANES), 1)
    q_tail = jnp.where(lane < 3, 1.0, 0.0).astype(BF16)
    k_pad = jnp.zeros((DH - SUBLANES, L), F32)
    ones_blk = _ones_column_block(L)

    for i in range(seq // L):
        r0 = i * L
        z = _dot(h_ref[r0:r0 + L, :], w_ref[...]) + b_ref[...]
        f_pre = z[:, 3 * W:3 * W + LANES].T[0:SUBLANES, :]
        cum_f = _prefix_sum_lanes(_log_sigmoid(f_pre), tri) + fc_ref[:, 0:1]
        fc_ref[...] = jnp.broadcast_to(cum_f[:, L - 1:L], (SUBLANES, LANES))
        f_pieces = _bf16_pieces(cum_f)
        row_max = []
        for hd in range(HEADS):
            lo = hd * DH
            p0, p1, p2 = (jnp.broadcast_to(-f[hd:hd + 1, :], (SUBLANES, L)) for f in f_pieces)
            bias_rows = jnp.where(sub == 0, p0, jnp.where(sub == 1, p1, jnp.where(sub == 2, p2, 0.0)))
            k_t = jnp.concatenate(
                [z[:, W + lo:W + lo + DH].T, bias_rows, k_pad], axis=0).astype(BF16)
            v_aug = jnp.concatenate(
                [z[:, 2 * W + lo:2 * W + lo + DH], ones_blk], axis=1).astype(BF16)
            kt_ref[hd, i] = k_t
            v_ref[hd, i] = v_aug
            q = jnp.concatenate(
                [(z[:, lo:lo + DH] * DH ** -0.5).astype(BF16), q_tail], axis=1)
            q_ref[hd] = q
            s = jnp.where(causal, _dot(q, k_t), -jnp.inf)
            m = jnp.max(s, axis=1, keepdims=True)
            acc_ref[hd] = _dot(jnp.exp(s - m).astype(BF16), v_aug)
            row_max.append(m)

        for j in range(i):
            for hd in range(HEADS):
                s = _dot(q_ref[hd], kt_ref[hd, j])
                m_new = jnp.maximum(row_max[hd], jnp.max(s, axis=1, keepdims=True))
                acc_ref[hd] = (jnp.exp(row_max[hd] - m_new) * acc_ref[hd]
                               + _dot(jnp.exp(s - m_new).astype(BF16), v_ref[hd, j]))
                row_max[hd] = m_new
        for hd in range(HEADS):
            acc = acc_ref[hd]
            o_ref[r0:r0 + L, hd * DH:(hd + 1) * DH] = (
                acc[:, :DH] / acc[:, DH:DH + 1]).astype(BF16)


def _fox(h2d, w, layer, b, batch, seq):
    d = h2d.shape[1]
    n = w.shape[2]
    nc = seq // CHUNK
    return pl.pallas_call(
        _fox_kernel,
        grid=(batch,),
        in_specs=[
            pl.BlockSpec((seq, d), lambda i: (i, 0)),
            _layer_spec((d, n), layer),
            _const_spec((1, n)),
        ],
        out_specs=pl.BlockSpec((seq, BRANCH_W), lambda i: (i, 0)),
        out_shape=jax.ShapeDtypeStruct((batch * seq, BRANCH_W), BF16),
        scratch_shapes=[
            pltpu.VMEM((HEADS, nc, 2 * DH, CHUNK), BF16),
            pltpu.VMEM((HEADS, nc, CHUNK, 2 * DH), BF16),
            pltpu.VMEM((SUBLANES, LANES), F32),
            pltpu.VMEM((HEADS, CHUNK, 2 * DH), BF16),
            pltpu.VMEM((HEADS, CHUNK, 2 * DH), F32),
        ],
        compiler_params=_params(("arbitrary",)),
        name="forgetting_attention",
    )(h2d, w, b)


def _merge_kernel(h_ref, ym_ref, yl_ref, yr_ref, yf_ref, x_ref, mod_ref, wg_ref, bg_ref,
                  wbr_ref, wo_ref, bo_ref, lng_ref, lnb_ref, o_ref, hn_ref, *, alpha):
    tm, d = x_ref.shape
    for r0 in range(0, tm, tm // MERGE_SPLIT):
        rows = slice(r0, r0 + tm // MERGE_SPLIT)
        h = h_ref[rows, :]
        merged = None
        for n, y_ref in enumerate((ym_ref, yl_ref, yr_ref, yf_ref)):
            gate = jax.nn.sigmoid(
                _dot(h, wg_ref[:, n * d:(n + 1) * d]) + bg_ref[:, n * d:(n + 1) * d])
            term = gate * _dot(y_ref[rows, :], wbr_ref[n])
            merged = term if merged is None else merged + term
        y = _dot(merged.astype(BF16), wo_ref[...]) + bo_ref[...]
        res = alpha * x_ref[rows, :] + (1.0 + mod_ref[2:3, :]) * y
        x_new = _layernorm(res) * lng_ref[...] + lnb_ref[...]
        o_ref[rows, :] = x_new
        hn_ref[rows, :] = _ln_modulated(x_new, mod_ref, shift_row=3, scale_row=4)


def _merge(h2d, ys, x2d, mod_l, wg, layer, bg, wbr, wo, bo, lng, lnb, seq, alpha):
    tokens, d = x2d.shape
    tm = ROW_TILE
    tiles_per_seq = seq // tm
    row = lambda i: (i, 0)
    return pl.pallas_call(
        functools.partial(_merge_kernel, alpha=alpha),
        grid=(tokens // tm,),
        in_specs=[
            pl.BlockSpec((tm, d), row),
            pl.BlockSpec((tm, BRANCH_W), row),
            pl.BlockSpec((tm, BRANCH_W), row),
            pl.BlockSpec((tm, BRANCH_W), row),
            pl.BlockSpec((tm, BRANCH_W), row),
            pl.BlockSpec((tm, d), row),
            pl.BlockSpec((None, 6, d), lambda i: (i // tiles_per_seq, 0, 0)),
            _layer_spec((d, N_BRANCH * d), layer),
            _const_spec((1, N_BRANCH * d)),
            _const_spec((N_BRANCH, BRANCH_W, d)),
            _const_spec((d, d)),
            _const_spec((1, d)),
            _const_spec((1, d)),
            _const_spec((1, d)),
        ],
        out_specs=[pl.BlockSpec((tm, d), row), pl.BlockSpec((tm, d), row)],
        out_shape=[jax.ShapeDtypeStruct((tokens, d), F32),
                   jax.ShapeDtypeStruct((tokens, d), BF16)],
        compiler_params=_params(("parallel",), VMEM_LIMIT_BYTES_WIDE),
        name="merge_out_proj",
    )(h2d, *ys, x2d, mod_l, wg, bg, wbr, wo, bo, lng, lnb)


def _mlp_kernel(h_ref, x_ref, mod_ref, w1_ref, b1_ref, w2_ref, b2_ref, lng_ref, lnb_ref,
                *rest, alpha, ff_tile, emit_next):
    d_ff = w1_ref.shape[1]
    tm = x_ref.shape[0]
    if emit_next:
        next_mod_ref, o_ref, hn_ref = rest
    else:
        (o_ref,) = rest
    for r0 in range(0, tm, tm // MERGE_SPLIT):
        rows = slice(r0, r0 + tm // MERGE_SPLIT)
        h = h_ref[rows, :]
        acc = None
        for c in range(d_ff // ff_tile):
            cols = slice(c * ff_tile, (c + 1) * ff_tile)
            a = jnp.maximum(_dot(h, w1_ref[:, cols]) + b1_ref[:, cols], 0.0)
            part = _dot((a * a).astype(BF16), w2_ref[cols, :])
            acc = part if acc is None else acc + part
        y = acc + b2_ref[...]
        res = alpha * x_ref[rows, :] + (1.0 + mod_ref[5:6, :]) * y
        x_new = _layernorm(res) * lng_ref[...] + lnb_ref[...]
        o_ref[rows, :] = x_new
        if emit_next:
            hn_ref[rows, :] = _ln_modulated(x_new, next_mod_ref, shift_row=0, scale_row=1)


def _mlp(h2d, x2d, mod_l, next_mod, w1, b1, w2, b2, lng, lnb, seq, alpha):
    tokens, d = x2d.shape
    d_ff = w1.shape[1]
    tm = ROW_TILE
    tiles_per_seq = seq // tm
    row = lambda i: (i, 0)
    mod_spec = pl.BlockSpec((None, 6, d), lambda i: (i // tiles_per_seq, 0, 0))
    emit_next = next_mod is not None
    in_specs = [
        pl.BlockSpec((tm, d), row),
        pl.BlockSpec((tm, d), row),
        mod_spec,
        _const_spec((d, d_ff)),
        _const_spec((1, d_ff)),
        _const_spec((d_ff, d)),
        _const_spec((1, d)),
        _const_spec((1, d)),
        _const_spec((1, d)),
    ]
    args = [h2d, x2d, mod_l, w1, b1, w2, b2, lng, lnb]
    out_specs = [pl.BlockSpec((tm, d), row)]
    out_shape = [jax.ShapeDtypeStruct((tokens, d), F32)]
    if emit_next:
        in_specs.append(mod_spec)
        args.append(next_mod)
        out_specs.append(pl.BlockSpec((tm, d), row))
        out_shape.append(jax.ShapeDtypeStruct((tokens, d), BF16))
    outs = pl.pallas_call(
        functools.partial(_mlp_kernel, alpha=alpha, ff_tile=d, emit_next=emit_next),
        grid=(tokens // tm,),
        in_specs=in_specs,
        out_specs=out_specs,
        out_shape=out_shape,
        compiler_params=_params(("parallel",), VMEM_LIMIT_BYTES_WIDE),
        name="sq_relu_mlp",
    )(*args)
    return (outs[0], outs[1]) if emit_next else (outs[0], None)


_PACK_GROUPS = (
    (("m_q", "m_k", "m_v", "m_o"), ("m_i", "m_f")),
    (("l_x", "l_g"), ()),
    (("r_q", "r_k", "r_v", "r_g"), ()),
    (("f_q", "f_k", "f_v"), ("f_f",)),
    (("gate",), ()),
)
def _pack_kernel(w_ref, o_ref, *, n_blocks, last_valid):
    depth, d = w_ref.shape[1], w_ref.shape[2]
    lane = lax.broadcasted_iota(jnp.int32, (d, LANES), 1)
    n_valid = jnp.where(pl.program_id(0) == n_blocks - 1, last_valid, LANES)
    for l in range(depth):
        o_ref[l] = jnp.where(lane < n_valid, w_ref[:, l, :].T, 0.0).astype(BF16)


def _pack_group(w_cols, seg, names, gate_names):
    n_in, depth, d = w_cols.shape
    start = seg[names[0]][0]
    main = seg[names[-1]][1] - start
    assert main % LANES == 0
    gate_w = sum(seg[n][1] - seg[n][0] for n in gate_names)
    assert not gate_names or seg[gate_names[0]][0] == start + main
    n_blocks = main // LANES + (1 if gate_names else 0)
    return pl.pallas_call(
        functools.partial(_pack_kernel, n_blocks=n_blocks,
                          last_valid=gate_w if gate_names else LANES),
        grid=(n_blocks,),
        in_specs=[pl.BlockSpec((pl.Element(LANES), pl.Element(depth), pl.Element(d)),
                               lambda j: (start + j * LANES, 0, 0))],
        out_specs=pl.BlockSpec((depth, d, LANES), lambda j: (0, 0, j)),
        out_shape=jax.ShapeDtypeStruct((depth, d, n_blocks * LANES), BF16),
        compiler_params=_params(("parallel",)),
        name="pack_in_proj",
    )(w_cols)


def _pack_weights(w_in, seg):
    w_cols = jnp.transpose(w_in, (2, 0, 1))
    return [_pack_group(w_cols, seg, names, gates) for names, gates in _PACK_GROUPS]


def _pack_bias(b_in_l, seg, names, gate_names):
    bs = [b_in_l[seg[n][0]:seg[n][1]] for n in names]
    if gate_names:
        gb = jnp.concatenate([b_in_l[seg[n][0]:seg[n][1]] for n in gate_names])
        bs.append(jnp.pad(gb, (0, LANES - gb.shape[0])))
    return jnp.concatenate(bs)[None, :]


def _block_diag(w):
    nb, bd, _ = w.shape
    eye = jnp.eye(nb, dtype=w.dtype)
    return (eye[:, None, :, None] * w[:, :, None, :]).reshape(nb * bd, nb * bd)


def kernel(x, c, positions, w_ada, b_ada, w_in, b_in, m_norm, conv_w, conv_b, lru_wa, lru_ba,
           lru_wx, lru_bx, lru_lam, r_norm, w_br, w_out, b_out, ln1_g, ln1_b, w_ff1, b_ff1,
           w_ff2, b_ff2, ln2_g, ln2_b):
    batch, seq, d = x.shape
    depth = w_ada.shape[0]
    assert seq % CHUNK == 0 and seq % ROW_TILE == 0 and d % LANES == 0
    alpha = (2 * depth) ** 0.25
    seg = _segment_slices(d)

    mod = _modulation(c, w_ada, b_ada).reshape(depth, batch, 6, d)
    cos_t, sin_t = _rope_tables(positions)
    x2d = x.reshape(batch * seq, d)
    packed = _pack_weights(w_in, seg)
    row = lambda v: v[None, :]
    w_m, w_l, w_r, w_f, w_g = packed
    h1 = _ln_modulate(x2d, mod[0], seq, shift_row=0, scale_row=1)
    for l in range(depth):
        b_m, b_l, b_r, b_f, b_g = (_pack_bias(b_in[l], seg, *grp) for grp in _PACK_GROUPS)
        y_m = _mlstm(h1, w_m, l, b_m, row(m_norm[l]), batch, seq)
        y_l = _rglru(h1, w_l, l, b_l, conv_w[l], row(conv_b[l]),
                     _block_diag(lru_wa[l]).astype(BF16), row(lru_ba[l]),
                     _block_diag(lru_wx[l]).astype(BF16), row(lru_bx[l]),
                     row(lru_lam[l]), batch, seq)
        y_r = _retention(h1, w_r, l, b_r, cos_t, sin_t, row(r_norm[l]), batch, seq)
        y_f = _fox(h1, w_f, l, b_f, batch, seq)
        x2d, h2 = _merge(h1, (y_m, y_l, y_r, y_f), x2d, mod[l], w_g, l, b_g,
                         w_br[l].astype(BF16), w_out[l].astype(BF16), row(b_out[l]),
                         row(ln1_g[l]), row(ln1_b[l]), seq, alpha)
        next_mod = mod[l + 1] if l + 1 < depth else None
        x2d, h1 = _mlp(h2, x2d, mod[l], next_mod, w_ff1[l].astype(BF16), row(b_ff1[l]),
                       w_ff2[l].astype(BF16), row(b_ff2[l]), row(ln2_g[l]), row(ln2_b[l]),
                       seq, alpha)
    return x2d.reshape(batch, seq, d)
```

```python
import functools
import math

import jax
import jax.numpy as jnp
from jax import lax
from jax.experimental import pallas as pl
from jax.experimental.pallas import tpu as pltpu

F32 = jnp.float32
BF16 = jnp.bfloat16

LN_EPS = 1e-5
HEADS = 4
DH = 128
BRANCH_W = HEADS * DH
LRU_BLOCKS = 8
LRU_C = 8.0
CONV_W = 4
ROPE_BASE = 10000.0
N_BRANCH = 4

LANES = 128
SUBLANES = 8
CHUNK = 256
ROW_TILE = 1024
MERGE_SPLIT = 4
VMEM_LIMIT_BYTES = 48 * 1024 * 1024
VMEM_LIMIT_BYTES_WIDE = 58 * 1024 * 1024

_SEGMENTS = (
    ("m_q", BRANCH_W), ("m_k", BRANCH_W), ("m_v", BRANCH_W), ("m_o", BRANCH_W),
    ("m_i", HEADS), ("m_f", HEADS),
    ("l_x", BRANCH_W), ("l_g", BRANCH_W),
    ("r_q", BRANCH_W), ("r_k", BRANCH_W), ("r_v", BRANCH_W), ("r_g", BRANCH_W),
    ("f_q", BRANCH_W), ("f_k", BRANCH_W), ("f_v", BRANCH_W), ("f_f", HEADS),
    ("gate", None),
)


def _segment_slices(d_model):
    out, acc = {}, 0
    for name, width in _SEGMENTS:
        width = N_BRANCH * d_model if width is None else width
        out[name] = (acc, acc + width)
        acc += width
    return out


def _dot(a, b):
    return jnp.dot(a, b, preferred_element_type=F32)


def _dot_f32(a, b):
    return jnp.dot(a, b, preferred_element_type=F32, precision=lax.Precision.HIGHEST)


def _layernorm(x):
    mu = jnp.mean(x, axis=-1, keepdims=True)
    xc = x - mu
    var = jnp.mean(xc * xc, axis=-1, keepdims=True)
    return xc * lax.rsqrt(var + LN_EPS)


def _log_sigmoid(x):
    return -(jnp.maximum(-x, 0.0) + jnp.log1p(jnp.exp(-jnp.abs(x))))


def _softplus(x):
    return jnp.maximum(x, 0.0) + jnp.log1p(jnp.exp(-jnp.abs(x)))


def _gelu_tanh(x):
    return x * (0.5 * (1.0 + jnp.tanh(math.sqrt(2.0 / math.pi) * (x + 0.044715 * (x * x * x)))))


def _causal_mask(n):
    row = lax.broadcasted_iota(jnp.int32, (n, n), 0)
    col = lax.broadcasted_iota(jnp.int32, (n, n), 1)
    return row >= col


def _prefix_sum_matrix(n, dtype):
    row = lax.broadcasted_iota(jnp.int32, (n, n), 0)
    col = lax.broadcasted_iota(jnp.int32, (n, n), 1)
    return jnp.where(row <= col, 1.0, 0.0).astype(dtype)


def _bf16_pieces(x):
    hi = x.astype(BF16).astype(F32)
    mid = (x - hi).astype(BF16).astype(F32)
    lo = (x - hi - mid).astype(BF16).astype(F32)
    return hi, mid, lo


def _prefix_sum_lanes(x, tri):
    rows = x.shape[0]
    stacked = jnp.concatenate(_bf16_pieces(x) + (jnp.zeros_like(x),), axis=0).astype(BF16)
    y = _dot(stacked, tri)
    return y[0:rows] + y[rows:2 * rows] + y[2 * rows:3 * rows]


def _ones_column_block(n):
    lane = lax.broadcasted_iota(jnp.int32, (n, LANES), 1)
    return jnp.where(lane == 0, 1.0, 0.0).astype(F32)


def _head_norm(y, gain):
    mu = jnp.mean(y, axis=-1, keepdims=True)
    yc = y - mu
    var = jnp.mean(yc * yc, axis=-1, keepdims=True)
    return yc * lax.rsqrt(var + LN_EPS) * gain


def _const_spec(shape):
    zeros = (0,) * len(shape)
    return pl.BlockSpec(shape, lambda *_: zeros, pipeline_mode=pl.Buffered(1))


def _layer_spec(shape, layer):
    idx = (layer,) + (0,) * len(shape)
    return pl.BlockSpec((None,) + tuple(shape), lambda *_: idx, pipeline_mode=pl.Buffered(1))


def _params(semantics, vmem_limit=VMEM_LIMIT_BYTES):
    return pltpu.CompilerParams(dimension_semantics=semantics, vmem_limit_bytes=vmem_limit)


def _mod_kernel(c_ref, w_ref, b_ref, o_ref):
    c = c_ref[...]
    cond = c * jax.nn.sigmoid(c)
    o_ref[...] = _dot_f32(cond, w_ref[...]) + b_ref[...]


def _modulation(c, w_ada, b_ada):
    depth, d, six_d = w_ada.shape
    batch = c.shape[0]
    n_tiles = six_d // d
    return pl.pallas_call(
        _mod_kernel,
        grid=(depth, n_tiles),
        in_specs=[
            pl.BlockSpec((batch, d), lambda l, j: (0, 0)),
            pl.BlockSpec((None, d, d), lambda l, j: (l, 0, j)),
            pl.BlockSpec((None, 1, d), lambda l, j: (l, 0, j)),
        ],
        out_specs=pl.BlockSpec((None, batch, d), lambda l, j: (l, 0, j)),
        out_shape=jax.ShapeDtypeStruct((depth, batch, six_d), F32),
        compiler_params=_params(("parallel", "parallel")),
        name="adaln_modulation",
    )(c, w_ada, b_ada.reshape(depth, 1, six_d))


def _rope_kernel(pos_ref, invf_ref, cos_ref, sin_ref):
    ang = invf_ref[...] * pos_ref[...].astype(F32)
    c = jnp.cos(ang)
    s = jnp.sin(ang)
    cos_ref[...] = jnp.concatenate([c, c], axis=0).T
    sin_ref[...] = jnp.concatenate([-s, s], axis=0).T


def _rope_tables(positions):
    batch, seq = positions.shape
    inv_freq = ROPE_BASE ** (-jnp.arange(0, DH, 2, dtype=F32) / DH)
    out = jax.ShapeDtypeStruct((batch, seq, DH), F32)
    return pl.pallas_call(
        _rope_kernel,
        grid=(batch,),
        in_specs=[
            pl.BlockSpec((None, 1, seq), lambda b: (b, 0, 0)),
            pl.BlockSpec((DH // 2, 1), lambda b: (0, 0)),
        ],
        out_specs=[pl.BlockSpec((None, seq, DH), lambda b: (b, 0, 0))] * 2,
        out_shape=[out, out],
        compiler_params=_params(("parallel",)),
        name="rope_tables",
    )(positions.reshape(batch, 1, seq), inv_freq.reshape(DH // 2, 1))


def _ln_modulated(x, mod_ref, shift_row, scale_row):
    scale = mod_ref[scale_row:scale_row + 1, :]
    shift = mod_ref[shift_row:shift_row + 1, :]
    return (_layernorm(x) * (1.0 + scale) + shift).astype(BF16)


def _ln_mod_kernel(x_ref, mod_ref, o_ref, *, shift_row, scale_row):
    o_ref[...] = _ln_modulated(x_ref[...], mod_ref, shift_row, scale_row)


def _ln_modulate(x2d, mod_l, seq, shift_row, scale_row):
    tokens, d = x2d.shape
    tm = ROW_TILE
    tiles_per_seq = seq // tm
    return pl.pallas_call(
        functools.partial(_ln_mod_kernel, shift_row=shift_row, scale_row=scale_row),
        grid=(tokens // tm,),
        in_specs=[
            pl.BlockSpec((tm, d), lambda i: (i, 0)),
            pl.BlockSpec((None, 6, d), lambda i: (i // tiles_per_seq, 0, 0)),
        ],
        out_specs=pl.BlockSpec((tm, d), lambda i: (i, 0)),
        out_shape=jax.ShapeDtypeStruct((tokens, d), BF16),
        compiler_params=_params(("parallel",)),
        name="ln_modulate",
    )(x2d, mod_l)


def _mlstm_kernel(h_ref, w_ref, b_ref, gain_ref, o_ref, state_ref, mc_ref, bc_ref, z_ref):
    seq = h_ref.shape[0]
    L = CHUNK
    nc = seq // L
    qkvo = HEADS * DH
    state_ref[...] = jnp.zeros_like(state_ref)
    mc_ref[...] = jnp.zeros_like(mc_ref)
    bc_ref[...] = jnp.zeros_like(bc_ref)
    tri = _prefix_sum_matrix(L, F32)

    def project(i):
        r0 = pl.multiple_of(i * L, L)
        z_ref[...] = _dot(h_ref[pl.ds(r0, L), :], w_ref[...]) + b_ref[...]

    def chunk(i, project_next):
        r0 = pl.multiple_of(i * L, L)
        z = z_ref[...]
        causal = _causal_mask(L)
        gates = z[:, 4 * qkvo:4 * qkvo + LANES].T[0:2 * HEADS, :]
        log_f = _log_sigmoid(gates)
        cum_f = _dot_f32(log_f, tri)
        ones_blk = _ones_column_block(L)
        heads = []
        for hd in range(HEADS):
            lo = hd * DH
            heads.append((
                z[:, lo:lo + DH].astype(BF16),
                (z[:, qkvo + lo:qkvo + lo + DH] * DH ** -0.5).T,
                jnp.concatenate(
                    [z[:, 2 * qkvo + lo:2 * qkvo + lo + DH], ones_blk], axis=1).astype(BF16),
                jax.nn.sigmoid(z[:, 3 * qkvo + lo:3 * qkvo + lo + DH])))
        if project_next:
            project(i + 1)
        for hd, (q, k_t, v_aug, o_gate) in enumerate(heads):
            lo = hd * DH
            mc = mc_ref[hd][:, 0:1]
            bc = bc_ref[hd][:, 0:1]
            lf_row = log_f[HEADS + hd:HEADS + hd + 1, :]
            b_row = cum_f[HEADS + hd:HEADS + hd + 1, :] + bc
            c_row = gates[hd:hd + 1, :] - b_row
            d0 = jnp.where(causal, c_row, -jnp.inf)
            m_t = jnp.maximum(jnp.max(d0, axis=1, keepdims=True), mc)
            p = jnp.exp(d0 - m_t)
            b_col = jnp.sum(jnp.where(causal, lf_row, 0.0), axis=1, keepdims=True) + bc
            s = _dot(q, k_t.astype(BF16))
            intra = _dot((s * p).astype(BF16), v_aug)
            st = state_ref[hd]
            inter = _dot(q, st.astype(BF16))
            tot = jnp.exp(mc - m_t) * inter + intra
            den = tot[:, DH:DH + 1]
            hh = tot[:, :DH] / jnp.maximum(jnp.abs(den), jnp.exp(-(b_col + m_t)))
            y = _head_norm(o_gate * hh, gain_ref[:, lo:lo + DH])
            o_ref[pl.ds(r0, L), lo:lo + DH] = y.astype(BF16)
            mc_new = m_t[L - 1:L, :]
            k_w = (k_t * jnp.exp(c_row - mc_new)).astype(BF16)
            state_ref[hd] = jnp.exp(mc - mc_new) * st + _dot(k_w, v_aug)
            mc_ref[hd] = jnp.broadcast_to(mc_new, (1, LANES))
            bc_ref[hd] = jnp.broadcast_to(b_row[:, L - 1:L], (1, LANES))

    project(0)
    for i in range(nc):
        chunk(i, project_next=i + 1 < nc)


def _mlstm(h2d, w, layer, b, gain, batch, seq):
    d = h2d.shape[1]
    n = b.shape[1]
    return pl.pallas_call(
        _mlstm_kernel,
        grid=(batch,),
        in_specs=[
            pl.BlockSpec((seq, d), lambda i: (i, 0)),
            _layer_spec((d, n), layer),
            _const_spec((1, n)),
            _const_spec((1, BRANCH_W)),
        ],
        out_specs=pl.BlockSpec((seq, BRANCH_W), lambda i: (i, 0)),
        out_shape=jax.ShapeDtypeStruct((batch * seq, BRANCH_W), BF16),
        scratch_shapes=[
            pltpu.VMEM((HEADS, DH, 2 * DH), F32),
            pltpu.VMEM((HEADS, 1, LANES), F32),
            pltpu.VMEM((HEADS, 1, LANES), F32),
            pltpu.VMEM((CHUNK, n), F32),
        ],
        compiler_params=_params(("arbitrary",)),
        name="mlstm",
    )(h2d, w, b, gain)


def _rglru_kernel(h_ref, w_ref, b_ref, cw_ref, cb_ref, wa_ref, ba_ref, wx_ref, bx_ref,
                  lam_ref, o_ref, xpad_ref, hbuf_ref, hcar_ref):
    seq = h_ref.shape[0]
    L = CHUNK
    R = BRANCH_W
    pad = SUBLANES
    xpad_ref[0:pad, :] = jnp.zeros((pad, R), F32)
    hcar_ref[...] = jnp.zeros_like(hcar_ref)

    groups = L // SUBLANES

    def step(i, carry):
        r0 = pl.multiple_of(i * L, L)
        z = _dot(h_ref[pl.ds(r0, L), :], w_ref[...]) + b_ref[...]
        xpad_ref[pad:pad + L, :] = z[:, :R]
        xp = xpad_ref[...].reshape(groups + 1, SUBLANES, R)
        xpad_ref[0:pad, :] = xp[groups]
        sub = lax.broadcasted_iota(jnp.int32, (groups, SUBLANES, R), 1)
        xc = cb_ref[...] + cw_ref[CONV_W - 1:CONV_W, :] * xp[1:]
        for k in range(1, CONV_W):
            rot = pltpu.roll(xp, k, 1)
            xc = xc + cw_ref[CONV_W - 1 - k:CONV_W - k, :] * jnp.where(sub >= k, rot[1:], rot[:-1])
        xc = xc.reshape(L, R)
        xcb = xc.astype(BF16)
        r = jax.nn.sigmoid(_dot(xcb, wa_ref[...]) + ba_ref[...])
        ig = jax.nn.sigmoid(_dot(xcb, wx_ref[...]) + bx_ref[...])
        log_a = (-LRU_C) * r * _softplus(-lam_ref[...])
        a = jnp.exp(log_a)
        th = jnp.tanh(log_a)
        u = jnp.exp(0.5 * jnp.log(-2.0 * th / (1.0 - th))) * (ig * xc)
        a = a.reshape(groups, SUBLANES, R)
        u = u.reshape(groups, SUBLANES, R)
        for sft in (1, 2, 4):
            ok = sub >= sft
            u = jnp.where(ok, a * pltpu.roll(u, sft, 1) + u, u)
            a = jnp.where(ok, a * pltpu.roll(a, sft, 1), a)
        hprev = hcar_ref[...]
        for g in range(groups):
            rows = slice(g * SUBLANES, (g + 1) * SUBLANES)
            hg = a[g] * hprev + u[g]
            hbuf_ref[rows, :] = hg
            hprev = hg[SUBLANES - 1:SUBLANES, :]
        hcar_ref[...] = hprev
        o_ref[pl.ds(r0, L), :] = (hbuf_ref[...] * _gelu_tanh(z[:, R:2 * R])).astype(BF16)
        return carry

    lax.fori_loop(0, seq // L, step, 0)


def _rglru(h2d, w, layer, b, conv_w, conv_b, wa, ba, wx, bx, lam, batch, seq):
    d = h2d.shape[1]
    R = BRANCH_W
    return pl.pallas_call(
        _rglru_kernel,
        grid=(batch,),
        in_specs=[
            pl.BlockSpec((seq, d), lambda i: (i, 0)),
            _layer_spec((d, 2 * R), layer),
            _const_spec((1, 2 * R)),
            _const_spec((CONV_W, R)),
            _const_spec((1, R)),
            _const_spec((R, R)),
            _const_spec((1, R)),
            _const_spec((R, R)),
            _const_spec((1, R)),
            _const_spec((1, R)),
        ],
        out_specs=pl.BlockSpec((seq, R), lambda i: (i, 0)),
        out_shape=jax.ShapeDtypeStruct((batch * seq, R), BF16),
        scratch_shapes=[
            pltpu.VMEM((CHUNK + SUBLANES, R), F32),
            pltpu.VMEM((CHUNK, R), F32),
            pltpu.VMEM((1, R), F32),
        ],
        compiler_params=_params(("arbitrary",)),
        name="rglru",
    )(h2d, w, b, conv_w, conv_b, wa, ba, wx, bx, lam)


def _log_gamma(hd):
    return math.log1p(-(2.0 ** (-5.0 - hd)))


def _retention_kernel(h_ref, w_ref, b_ref, cos_ref, sin_ref, gain_ref, o_ref, state_ref,
                      decay_ref, z_ref):
    seq = h_ref.shape[0]
    L = CHUNK
    nc = seq // L
    W = HEADS * DH
    state_ref[...] = jnp.zeros_like(state_ref)
    row = lax.broadcasted_iota(jnp.int32, (L, L), 0)
    col = lax.broadcasted_iota(jnp.int32, (L, L), 1)
    rel = (row - col).astype(F32)
    for hd in range(HEADS):
        decay_ref[hd] = jnp.where(rel >= 0.0, jnp.exp(rel * _log_gamma(hd)), 0.0)

    def project(i):
        r0 = pl.multiple_of(i * L, L)
        z_ref[...] = _dot(h_ref[pl.ds(r0, L), :], w_ref[...]) + b_ref[...]

    def chunk(i, project_next):
        r0 = pl.multiple_of(i * L, L)
        z = z_ref[...]
        cosf = cos_ref[pl.ds(r0, L), :]
        sinf = sin_ref[pl.ds(r0, L), :]
        idx_col = lax.broadcasted_iota(jnp.int32, (L, 1), 0).astype(F32)
        idx_row = lax.broadcasted_iota(jnp.int32, (1, L), 1).astype(F32)
        heads = []
        for hd in range(HEADS):
            lo = hd * DH
            qf = z[:, lo:lo + DH]
            kf = z[:, W + lo:W + lo + DH]
            heads.append((
                (qf * cosf + pltpu.roll(qf, DH // 2, 1) * sinf).astype(BF16),
                ((kf * cosf + pltpu.roll(kf, DH // 2, 1) * sinf) * DH ** -0.5).T,
                z[:, 2 * W + lo:2 * W + lo + DH].astype(BF16),
                z[:, 3 * W + lo:3 * W + lo + DH]))
        if project_next:
            project(i + 1)
        for hd, (q, k_t, v, g) in enumerate(heads):
            lo = hd * DH
            lg = _log_gamma(hd)
            s = _dot(q, k_t.astype(BF16)) * decay_ref[hd]
            st = state_ref[hd]
            out = _dot(s.astype(BF16), v) + jnp.exp((idx_col + 1.0) * lg) * _dot(q, st.astype(BF16))
            k_w = (k_t * jnp.exp((L - 1.0 - idx_row) * lg)).astype(BF16)
            state_ref[hd] = math.exp(L * lg) * st + _dot(k_w, v)
            y = (g * jax.nn.sigmoid(g)) * _head_norm(out, gain_ref[:, lo:lo + DH])
            o_ref[pl.ds(r0, L), lo:lo + DH] = y.astype(BF16)

    project(0)
    for i in range(nc):
        chunk(i, project_next=i + 1 < nc)


def _retention(h2d, w, layer, b, cos_t, sin_t, gain, batch, seq):
    d = h2d.shape[1]
    n = b.shape[1]
    return pl.pallas_call(
        _retention_kernel,
        grid=(batch,),
        in_specs=[
            pl.BlockSpec((seq, d), lambda i: (i, 0)),
            _layer_spec((d, n), layer),
            _const_spec((1, n)),
            pl.BlockSpec((None, seq, DH), lambda i: (i, 0, 0)),
            pl.BlockSpec((None, seq, DH), lambda i: (i, 0, 0)),
            _const_spec((1, BRANCH_W)),
        ],
        out_specs=pl.BlockSpec((seq, BRANCH_W), lambda i: (i, 0)),
        out_shape=jax.ShapeDtypeStruct((batch * seq, BRANCH_W), BF16),
        scratch_shapes=[
            pltpu.VMEM((HEADS, DH, DH), F32),
            pltpu.VMEM((HEADS, CHUNK, CHUNK), F32),
            pltpu.VMEM((CHUNK, n), F32),
        ],
        compiler_params=_params(("arbitrary",)),
        name="retention",
    )(h2d, w, b, cos_t, sin_t, gain)


def _fox_kernel(h_ref, w_ref, b_ref, o_ref, kt_ref, v_ref, fc_ref, q_ref, acc_ref):
    seq = h_ref.shape[0]
    L = CHUNK
    W = HEADS * DH
    fc_ref[...] = jnp.zeros_like(fc_ref)
    tri = _prefix_sum_matrix(L, BF16)
    causal = _causal_mask(L)
    sub = lax.broadcasted_iota(jnp.int32, (SUBLANES, L), 0)
    lane = lax.broadcasted_iota(jnp.int32, (L, LANES), 1)
    q_tail = jnp.where(lane < 3, 1.0, 0.0).astype(BF16)
    k_pad = jnp.zeros((DH - SUBLANES, L), F32)
    ones_blk = _ones_column_block(L)

    for i in range(seq // L):
        r0 = i * L
        z = _dot(h_ref[r0:r0 + L, :], w_ref[...]) + b_ref[...]
        f_pre = z[:, 3 * W:3 * W + LANES].T[0:SUBLANES, :]
        cum_f = _prefix_sum_lanes(_log_sigmoid(f_pre), tri) + fc_ref[:, 0:1]
        fc_ref[...] = jnp.broadcast_to(cum_f[:, L - 1:L], (SUBLANES, LANES))
        f_pieces = _bf16_pieces(cum_f)
        row_max = []
        for hd in range(HEADS):
            lo = hd * DH
            p0, p1, p2 = (jnp.broadcast_to(-f[hd:hd + 1, :], (SUBLANES, L)) for f in f_pieces)
            bias_rows = jnp.where(sub == 0, p0, jnp.where(sub == 1, p1, jnp.where(sub == 2, p2, 0.0)))
            k_t = jnp.concatenate(
                [z[:, W + lo:W + lo + DH].T, bias_rows, k_pad], axis=0).astype(BF16)
            v_aug = jnp.concatenate(
                [z[:, 2 * W + lo:2 * W + lo + DH], ones_blk], axis=1).astype(BF16)
            kt_ref[hd, i] = k_t
            v_ref[hd, i] = v_aug
            q = jnp.concatenate(
                [(z[:, lo:lo + DH] * DH ** -0.5).astype(BF16), q_tail], axis=1)
            q_ref[hd] = q
            s = jnp.where(causal, _dot(q, k_t), -jnp.inf)
            m = jnp.max(s, axis=1, keepdims=True)
            acc_ref[hd] = _dot(jnp.exp(s - m).astype(BF16), v_aug)
            row_max.append(m)

        for j in range(i):
            for hd in range(HEADS):
                s = _dot(q_ref[hd], kt_ref[hd, j])
                m_new = jnp.maximum(row_max[hd], jnp.max(s, axis=1, keepdims=True))
                acc_ref[hd] = (jnp.exp(row_max[hd] - m_new) * acc_ref[hd]
                               + _dot(jnp.exp(s - m_new).astype(BF16), v_ref[hd, j]))
                row_max[hd] = m_new
        for hd in range(HEADS):
            acc = acc_ref[hd]
            o_ref[r0:r0 + L, hd * DH:(hd + 1) * DH] = (
                acc[:, :DH] / acc[:, DH:DH + 1]).astype(BF16)


def _fox(h2d, w, layer, b, batch, seq):
    d = h2d.shape[1]
    n = b.shape[1]
    nc = seq // CHUNK
    return pl.pallas_call(
        _fox_kernel,
        grid=(batch,),
        in_specs=[
            pl.BlockSpec((seq, d), lambda i: (i, 0)),
            _layer_spec((d, n), layer),
            _const_spec((1, n)),
        ],
        out_specs=pl.BlockSpec((seq, BRANCH_W), lambda i: (i, 0)),
        out_shape=jax.ShapeDtypeStruct((batch * seq, BRANCH_W), BF16),
        scratch_shapes=[
            pltpu.VMEM((HEADS, nc, 2 * DH, CHUNK), BF16),
            pltpu.VMEM((HEADS, nc, CHUNK, 2 * DH), BF16),
            pltpu.VMEM((SUBLANES, LANES), F32),
            pltpu.VMEM((HEADS, CHUNK, 2 * DH), BF16),
            pltpu.VMEM((HEADS, CHUNK, 2 * DH), F32),
        ],
        compiler_params=_params(("arbitrary",)),
        name="forgetting_attention",
    )(h2d, w, b)


def _merge_kernel(h_ref, ym_ref, yl_ref, yr_ref, yf_ref, x_ref, mod_ref, wg_ref, bg_ref,
                  wbr_ref, wo_ref, bo_ref, lng_ref, lnb_ref, o_ref, hn_ref, *, alpha):
    tm, d = x_ref.shape
    for r0 in range(0, tm, tm // MERGE_SPLIT):
        rows = slice(r0, r0 + tm // MERGE_SPLIT)
        h = h_ref[rows, :]
        merged = None
        for n, y_ref in enumerate((ym_ref, yl_ref, yr_ref, yf_ref)):
            gate = jax.nn.sigmoid(
                _dot(h, wg_ref[:, n * d:(n + 1) * d]) + bg_ref[:, n * d:(n + 1) * d])
            term = gate * _dot(y_ref[rows, :], wbr_ref[n])
            merged = term if merged is None else merged + term
        y = _dot(merged.astype(BF16), wo_ref[...]) + bo_ref[...]
        res = alpha * x_ref[rows, :] + (1.0 + mod_ref[2:3, :]) * y
        x_new = _layernorm(res) * lng_ref[...] + lnb_ref[...]
        o_ref[rows, :] = x_new
        hn_ref[rows, :] = _ln_modulated(x_new, mod_ref, shift_row=3, scale_row=4)


def _merge(h2d, ys, x2d, mod_l, wg, layer, bg, wbr, wo, bo, lng, lnb, seq, alpha):
    tokens, d = x2d.shape
    tm = ROW_TILE
    tiles_per_seq = seq // tm
    row = lambda i: (i, 0)
    return pl.pallas_call(
        functools.partial(_merge_kernel, alpha=alpha),
        grid=(tokens // tm,),
        in_specs=[
            pl.BlockSpec((tm, d), row),
            pl.BlockSpec((tm, BRANCH_W), row),
            pl.BlockSpec((tm, BRANCH_W), row),
            pl.BlockSpec((tm, BRANCH_W), row),
            pl.BlockSpec((tm, BRANCH_W), row),
            pl.BlockSpec((tm, d), row),
            pl.BlockSpec((None, 6, d), lambda i: (i // tiles_per_seq, 0, 0)),
            _layer_spec((d, N_BRANCH * d), layer),
            _const_spec((1, N_BRANCH * d)),
            _const_spec((N_BRANCH, BRANCH_W, d)),
            _const_spec((d, d)),
            _const_spec((1, d)),
            _const_spec((1, d)),
            _const_spec((1, d)),
        ],
        out_specs=[pl.BlockSpec((tm, d), row), pl.BlockSpec((tm, d), row)],
        out_shape=[jax.ShapeDtypeStruct((tokens, d), F32),
                   jax.ShapeDtypeStruct((tokens, d), BF16)],
        compiler_params=_params(("parallel",), VMEM_LIMIT_BYTES_WIDE),
        name="merge_out_proj",
    )(h2d, *ys, x2d, mod_l, wg, bg, wbr, wo, bo, lng, lnb)


def _mlp_kernel(h_ref, x_ref, mod_ref, w1_ref, b1_ref, w2_ref, b2_ref, lng_ref, lnb_ref,
                *rest, alpha, ff_tile, emit_next):
    d_ff = w1_ref.shape[1]
    tm = x_ref.shape[0]
    if emit_next:
        next_mod_ref, o_ref, hn_ref = rest
    else:
        (o_ref,) = rest
    for r0 in range(0, tm, tm // MERGE_SPLIT):
        rows = slice(r0, r0 + tm // MERGE_SPLIT)
        h = h_ref[rows, :]
        acc = None
        for c in range(d_ff // ff_tile):
            cols = slice(c * ff_tile, (c + 1) * ff_tile)
            a = jnp.maximum(_dot(h, w1_ref[:, cols]) + b1_ref[:, cols], 0.0)
            part = _dot((a * a).astype(BF16), w2_ref[cols, :])
            acc = part if acc is None else acc + part
        y = acc + b2_ref[...]
        res = alpha * x_ref[rows, :] + (1.0 + mod_ref[5:6, :]) * y
        x_new = _layernorm(res) * lng_ref[...] + lnb_ref[...]
        o_ref[rows, :] = x_new
        if emit_next:
            hn_ref[rows, :] = _ln_modulated(x_new, next_mod_ref, shift_row=0, scale_row=1)


def _mlp(h2d, x2d, mod_l, next_mod, w1, b1, w2, b2, lng, lnb, seq, alpha):
    tokens, d = x2d.shape
    d_ff = w1.shape[1]
    tm = ROW_TILE
    tiles_per_seq = seq // tm
    row = lambda i: (i, 0)
    mod_spec = pl.BlockSpec((None, 6, d), lambda i: (i // tiles_per_seq, 0, 0))
    emit_next = next_mod is not None
    in_specs = [
        pl.BlockSpec((tm, d), row),
        pl.BlockSpec((tm, d), row),
        mod_spec,
        _const_spec((d, d_ff)),
        _const_spec((1, d_ff)),
        _const_spec((d_ff, d)),
        _const_spec((1, d)),
        _const_spec((1, d)),
        _const_spec((1, d)),
    ]
    args = [h2d, x2d, mod_l, w1, b1, w2, b2, lng, lnb]
    out_specs = [pl.BlockSpec((tm, d), row)]
    out_shape = [jax.ShapeDtypeStruct((tokens, d), F32)]
    if emit_next:
        in_specs.append(mod_spec)
        args.append(next_mod)
        out_specs.append(pl.BlockSpec((tm, d), row))
        out_shape.append(jax.ShapeDtypeStruct((tokens, d), BF16))
    outs = pl.pallas_call(
        functools.partial(_mlp_kernel, alpha=alpha, ff_tile=d, emit_next=emit_next),
        grid=(tokens // tm,),
        in_specs=in_specs,
        out_specs=out_specs,
        out_shape=out_shape,
        compiler_params=_params(("parallel",), VMEM_LIMIT_BYTES_WIDE),
        name="sq_relu_mlp",
    )(*args)
    return (outs[0], outs[1]) if emit_next else (outs[0], None)


_PACK_GROUPS = {
    "mlstm": (("m_q", "m_k", "m_v", "m_o"), ("m_i", "m_f")),
    "rglru": (("l_x", "l_g"), ()),
    "retention": (("r_q", "r_k", "r_v", "r_g"), ()),
    "fox": (("f_q", "f_k", "f_v"), ("f_f",)),
    "gate": (("gate",), ()),
}
PACK_COLS = 512


def _pack_kernel(w_ref, o_ref, *, valid):
    depth, d = w_ref.shape[1], w_ref.shape[2]
    col = pl.program_id(0) * PACK_COLS + lax.broadcasted_iota(jnp.int32, (d, PACK_COLS), 1)
    for l in range(depth):
        o_ref[l] = jnp.where(col < valid, w_ref[:, l, :].T, 0.0).astype(BF16)


def _pack_group(w_cols, seg, names, gate_names):
    n_in, depth, d = w_cols.shape
    start = seg[names[0]][0]
    main = seg[names[-1]][1] - start
    assert main % LANES == 0
    gate_w = sum(seg[n][1] - seg[n][0] for n in gate_names)
    assert not gate_names or seg[gate_names[0]][0] == start + main
    n_blocks = pl.cdiv(main + (LANES if gate_names else 0), PACK_COLS)
    assert start + n_blocks * PACK_COLS <= n_in
    return pl.pallas_call(
        functools.partial(_pack_kernel, valid=main + gate_w),
        grid=(n_blocks,),
        in_specs=[pl.BlockSpec((pl.Element(PACK_COLS), pl.Element(depth), pl.Element(d)),
                               lambda j: (start + j * PACK_COLS, 0, 0))],
        out_specs=pl.BlockSpec((depth, d, PACK_COLS), lambda j: (0, 0, j)),
        out_shape=jax.ShapeDtypeStruct((depth, d, n_blocks * PACK_COLS), BF16),
        compiler_params=_params(("parallel",)),
        name="pack_in_proj",
    )(w_cols)


def _pack_weights(w_in, seg):
    w_cols = jnp.transpose(w_in, (2, 0, 1))
    return {k: _pack_group(w_cols, seg, *grp) for k, grp in _PACK_GROUPS.items()}


def _pack_bias(b_in_l, seg, key):
    names, gate_names = _PACK_GROUPS[key]
    bs = [b_in_l[seg[n][0]:seg[n][1]] for n in names]
    if gate_names:
        gb = jnp.concatenate([b_in_l[seg[n][0]:seg[n][1]] for n in gate_names])
        bs.append(jnp.pad(gb, (0, LANES - gb.shape[0])))
    return jnp.concatenate(bs)[None, :]


def _block_diag(w):
    nb, bd, _ = w.shape
    eye = jnp.eye(nb, dtype=w.dtype)
    return (eye[:, None, :, None] * w[:, :, None, :]).reshape(nb * bd, nb * bd)


def kernel(x, c, positions, w_ada, b_ada, w_in, b_in, m_norm, conv_w, conv_b, lru_wa, lru_ba,
           lru_wx, lru_bx, lru_lam, r_norm, w_br, w_out, b_out, ln1_g, ln1_b, w_ff1, b_ff1,
           w_ff2, b_ff2, ln2_g, ln2_b):
    batch, seq, d = x.shape
    depth = w_ada.shape[0]
    assert seq % CHUNK == 0 and seq % ROW_TILE == 0 and d % LANES == 0
    alpha = (2 * depth) ** 0.25
    seg = _segment_slices(d)

    mod = _modulation(c, w_ada, b_ada).reshape(depth, batch, 6, d)
    cos_t, sin_t = _rope_tables(positions)
    x2d = x.reshape(batch * seq, d)
    w = _pack_weights(w_in, seg)
    row = lambda v: v[None, :]
    h1 = _ln_modulate(x2d, mod[0], seq, shift_row=0, scale_row=1)
    for l in range(depth):
        bias = lambda key: _pack_bias(b_in[l], seg, key)
        y_m = _mlstm(h1, w["mlstm"], l, bias("mlstm"), row(m_norm[l]), batch, seq)
        y_l = _rglru(h1, w["rglru"], l, bias("rglru"), conv_w[l], row(conv_b[l]),
                     _block_diag(lru_wa[l]).astype(BF16), row(lru_ba[l]),
                     _block_diag(lru_wx[l]).astype(BF16), row(lru_bx[l]),
                     row(lru_lam[l]), batch, seq)
        y_r = _retention(h1, w["retention"], l, bias("retention"), cos_t, sin_t,
                         row(r_norm[l]), batch, seq)
        y_f = _fox(h1, w["fox"], l, bias("fox"), batch, seq)
        x2d, h2 = _merge(h1, (y_m, y_l, y_r, y_f), x2d, mod[l], w["gate"], l, bias("gate"),
                         w_br[l].astype(BF16), w_out[l].astype(BF16), row(b_out[l]),
                         row(ln1_g[l]), row(ln1_b[l]), seq, alpha)
        next_mod = mod[l + 1] if l + 1 < depth else None
        x2d, h1 = _mlp(h2, x2d, mod[l], next_mod, w_ff1[l].astype(BF16), row(b_ff1[l]),
                       w_ff2[l].astype(BF16), row(b_ff2[l]), row(ln2_g[l]), row(ln2_b[l]),
                       seq, alpha)
    return x2d.reshape(batch, seq, d)
```

```python
import functools
import math

import jax
import jax.numpy as jnp
from jax import lax
from jax.experimental import pallas as pl
from jax.experimental.pallas import tpu as pltpu

F32 = jnp.float32
BF16 = jnp.bfloat16

LN_EPS = 1e-5
HEADS = 4
DH = 128
BRANCH_W = HEADS * DH
LRU_BLOCKS = 8
LRU_C = 8.0
CONV_W = 4
ROPE_BASE = 10000.0
N_BRANCH = 4

LANES = 128
SUBLANES = 8
CHUNK = 256
ROW_TILE = 1024
MERGE_SPLIT = 4
VMEM_LIMIT_BYTES = 48 * 1024 * 1024
VMEM_LIMIT_BYTES_WIDE = 58 * 1024 * 1024

_SEGMENTS = (
    ("m_q", BRANCH_W), ("m_k", BRANCH_W), ("m_v", BRANCH_W), ("m_o", BRANCH_W),
    ("m_i", HEADS), ("m_f", HEADS),
    ("l_x", BRANCH_W), ("l_g", BRANCH_W),
    ("r_q", BRANCH_W), ("r_k", BRANCH_W), ("r_v", BRANCH_W), ("r_g", BRANCH_W),
    ("f_q", BRANCH_W), ("f_k", BRANCH_W), ("f_v", BRANCH_W), ("f_f", HEADS),
    ("gate", None),
)


def _segment_slices(d_model):
    out, acc = {}, 0
    for name, width in _SEGMENTS:
        width = N_BRANCH * d_model if width is None else width
        out[name] = (acc, acc + width)
        acc += width
    return out


def _dot(a, b):
    return jnp.dot(a, b, preferred_element_type=F32)


def _dot_f32(a, b):
    return jnp.dot(a, b, preferred_element_type=F32, precision=lax.Precision.HIGHEST)


def _layernorm(x):
    mu = jnp.mean(x, axis=-1, keepdims=True)
    xc = x - mu
    var = jnp.mean(xc * xc, axis=-1, keepdims=True)
    return xc * lax.rsqrt(var + LN_EPS)


def _log_sigmoid(x):
    return -(jnp.maximum(-x, 0.0) + jnp.log1p(jnp.exp(-jnp.abs(x))))


def _softplus(x):
    return jnp.maximum(x, 0.0) + jnp.log1p(jnp.exp(-jnp.abs(x)))


def _gelu_tanh(x):
    return x * (0.5 * (1.0 + jnp.tanh(math.sqrt(2.0 / math.pi) * (x + 0.044715 * (x * x * x)))))


def _causal_mask(n):
    row = lax.broadcasted_iota(jnp.int32, (n, n), 0)
    col = lax.broadcasted_iota(jnp.int32, (n, n), 1)
    return row >= col


def _prefix_sum_matrix(n, dtype):
    row = lax.broadcasted_iota(jnp.int32, (n, n), 0)
    col = lax.broadcasted_iota(jnp.int32, (n, n), 1)
    return jnp.where(row <= col, 1.0, 0.0).astype(dtype)


def _bf16_pieces(x):
    hi = x.astype(BF16).astype(F32)
    mid = (x - hi).astype(BF16).astype(F32)
    lo = (x - hi - mid).astype(BF16).astype(F32)
    return hi, mid, lo


def _prefix_sum_lanes(x, tri):
    rows = x.shape[0]
    stacked = jnp.concatenate(_bf16_pieces(x) + (jnp.zeros_like(x),), axis=0).astype(BF16)
    y = _dot(stacked, tri)
    return y[0:rows] + y[rows:2 * rows] + y[2 * rows:3 * rows]


def _ones_column_block(n):
    lane = lax.broadcasted_iota(jnp.int32, (n, LANES), 1)
    return jnp.where(lane == 0, 1.0, 0.0).astype(F32)


def _head_norm(y, gain):
    mu = jnp.mean(y, axis=-1, keepdims=True)
    yc = y - mu
    var = jnp.mean(yc * yc, axis=-1, keepdims=True)
    return yc * lax.rsqrt(var + LN_EPS) * gain


def _const_spec(shape):
    zeros = (0,) * len(shape)
    return pl.BlockSpec(shape, lambda *_: zeros, pipeline_mode=pl.Buffered(1))


def _layer_spec(shape, layer):
    idx = (layer,) + (0,) * len(shape)
    return pl.BlockSpec((None,) + tuple(shape), lambda *_: idx, pipeline_mode=pl.Buffered(1))


def _params(semantics, vmem_limit=VMEM_LIMIT_BYTES):
    return pltpu.CompilerParams(dimension_semantics=semantics, vmem_limit_bytes=vmem_limit)


def _mod_kernel(c_ref, w_ref, b_ref, o_ref):
    c = c_ref[...]
    cond = c * jax.nn.sigmoid(c)
    o_ref[...] = _dot_f32(cond, w_ref[...]) + b_ref[...]


def _modulation(c, w_ada, b_ada):
    depth, d, six_d = w_ada.shape
    batch = c.shape[0]
    n_tiles = six_d // d
    return pl.pallas_call(
        _mod_kernel,
        grid=(depth, n_tiles),
        in_specs=[
            pl.BlockSpec((batch, d), lambda l, j: (0, 0)),
            pl.BlockSpec((None, d, d), lambda l, j: (l, 0, j)),
            pl.BlockSpec((None, 1, d), lambda l, j: (l, 0, j)),
        ],
        out_specs=pl.BlockSpec((None, batch, d), lambda l, j: (l, 0, j)),
        out_shape=jax.ShapeDtypeStruct((depth, batch, six_d), F32),
        compiler_params=_params(("parallel", "parallel")),
        name="adaln_modulation",
    )(c, w_ada, b_ada.reshape(depth, 1, six_d))


def _rope_kernel(pos_ref, invf_ref, cos_ref, sin_ref):
    ang = invf_ref[...] * pos_ref[...].astype(F32)
    c = jnp.cos(ang)
    s = jnp.sin(ang)
    cos_ref[...] = jnp.concatenate([c, c], axis=0).T
    sin_ref[...] = jnp.concatenate([-s, s], axis=0).T


def _rope_tables(positions):
    batch, seq = positions.shape
    inv_freq = ROPE_BASE ** (-jnp.arange(0, DH, 2, dtype=F32) / DH)
    out = jax.ShapeDtypeStruct((batch, seq, DH), F32)
    return pl.pallas_call(
        _rope_kernel,
        grid=(batch,),
        in_specs=[
            pl.BlockSpec((None, 1, seq), lambda b: (b, 0, 0)),
            pl.BlockSpec((DH // 2, 1), lambda b: (0, 0)),
        ],
        out_specs=[pl.BlockSpec((None, seq, DH), lambda b: (b, 0, 0))] * 2,
        out_shape=[out, out],
        compiler_params=_params(("parallel",)),
        name="rope_tables",
    )(positions.reshape(batch, 1, seq), inv_freq.reshape(DH // 2, 1))


def _ln_modulated(x, mod_ref, shift_row, scale_row):
    scale = mod_ref[scale_row:scale_row + 1, :]
    shift = mod_ref[shift_row:shift_row + 1, :]
    return (_layernorm(x) * (1.0 + scale) + shift).astype(BF16)


def _ln_mod_kernel(x_ref, mod_ref, o_ref, *, shift_row, scale_row):
    o_ref[...] = _ln_modulated(x_ref[...], mod_ref, shift_row, scale_row)


def _ln_modulate(x2d, mod_l, seq, shift_row, scale_row):
    tokens, d = x2d.shape
    tm = ROW_TILE
    tiles_per_seq = seq // tm
    return pl.pallas_call(
        functools.partial(_ln_mod_kernel, shift_row=shift_row, scale_row=scale_row),
        grid=(tokens // tm,),
        in_specs=[
            pl.BlockSpec((tm, d), lambda i: (i, 0)),
            pl.BlockSpec((None, 6, d), lambda i: (i // tiles_per_seq, 0, 0)),
        ],
        out_specs=pl.BlockSpec((tm, d), lambda i: (i, 0)),
        out_shape=jax.ShapeDtypeStruct((tokens, d), BF16),
        compiler_params=_params(("parallel",)),
        name="ln_modulate",
    )(x2d, mod_l)


def _mlstm_kernel(h_ref, w_ref, b_ref, gain_ref, o_ref, state_ref, mc_ref, bc_ref, z_ref):
    seq = h_ref.shape[0]
    L = CHUNK
    nc = seq // L
    qkvo = HEADS * DH
    state_ref[...] = jnp.zeros_like(state_ref)
    mc_ref[...] = jnp.zeros_like(mc_ref)
    bc_ref[...] = jnp.zeros_like(bc_ref)
    tri = _prefix_sum_matrix(L, F32)

    def project(i):
        r0 = pl.multiple_of(i * L, L)
        z_ref[...] = _dot(h_ref[pl.ds(r0, L), :], w_ref[...]) + b_ref[...]

    def chunk(i, project_next):
        r0 = pl.multiple_of(i * L, L)
        z = z_ref[...]
        causal = _causal_mask(L)
        gates = z[:, 4 * qkvo:4 * qkvo + LANES].T[0:2 * HEADS, :]
        log_f = _log_sigmoid(gates)
        cum_f = _dot_f32(log_f, tri)
        ones_blk = _ones_column_block(L)
        heads = []
        for hd in range(HEADS):
            lo = hd * DH
            heads.append((
                z[:, lo:lo + DH].astype(BF16),
                (z[:, qkvo + lo:qkvo + lo + DH] * DH ** -0.5).T,
                jnp.concatenate(
                    [z[:, 2 * qkvo + lo:2 * qkvo + lo + DH], ones_blk], axis=1).astype(BF16),
                jax.nn.sigmoid(z[:, 3 * qkvo + lo:3 * qkvo + lo + DH])))
        if project_next:
            project(i + 1)
        for hd, (q, k_t, v_aug, o_gate) in enumerate(heads):
            lo = hd * DH
            mc = mc_ref[hd][:, 0:1]
            bc = bc_ref[hd][:, 0:1]
            lf_row = log_f[HEADS + hd:HEADS + hd + 1, :]
            b_row = cum_f[HEADS + hd:HEADS + hd + 1, :] + bc
            c_row = gates[hd:hd + 1, :] - b_row
            d0 = jnp.where(causal, c_row, -jnp.inf)
            m_t = jnp.maximum(jnp.max(d0, axis=1, keepdims=True), mc)
            p = jnp.exp(d0 - m_t)
            b_col = jnp.sum(jnp.where(causal, lf_row, 0.0), axis=1, keepdims=True) + bc
            s = _dot(q, k_t.astype(BF16))
            intra = _dot((s * p).astype(BF16), v_aug)
            st = state_ref[hd]
            inter = _dot(q, st.astype(BF16))
            tot = jnp.exp(mc - m_t) * inter + intra
            den = tot[:, DH:DH + 1]
            hh = tot[:, :DH] / jnp.maximum(jnp.abs(den), jnp.exp(-(b_col + m_t)))
            y = _head_norm(o_gate * hh, gain_ref[:, lo:lo + DH])
            o_ref[pl.ds(r0, L), lo:lo + DH] = y.astype(BF16)
            mc_new = m_t[L - 1:L, :]
            k_w = (k_t * jnp.exp(c_row - mc_new)).astype(BF16)
            state_ref[hd] = jnp.exp(mc - mc_new) * st + _dot(k_w, v_aug)
            mc_ref[hd] = jnp.broadcast_to(mc_new, (1, LANES))
            bc_ref[hd] = jnp.broadcast_to(b_row[:, L - 1:L], (1, LANES))

    project(0)
    for i in range(nc):
        chunk(i, project_next=i + 1 < nc)


def _mlstm(h2d, w, layer, b, gain, batch, seq):
    d = h2d.shape[1]
    n = w.shape[2]
    return pl.pallas_call(
        _mlstm_kernel,
        grid=(batch,),
        in_specs=[
            pl.BlockSpec((seq, d), lambda i: (i, 0)),
            _layer_spec((d, n), layer),
            _const_spec((1, n)),
            _const_spec((1, BRANCH_W)),
        ],
        out_specs=pl.BlockSpec((seq, BRANCH_W), lambda i: (i, 0)),
        out_shape=jax.ShapeDtypeStruct((batch * seq, BRANCH_W), BF16),
        scratch_shapes=[
            pltpu.VMEM((HEADS, DH, 2 * DH), F32),
            pltpu.VMEM((HEADS, 1, LANES), F32),
            pltpu.VMEM((HEADS, 1, LANES), F32),
            pltpu.VMEM((CHUNK, n), F32),
        ],
        compiler_params=_params(("arbitrary",)),
        name="mlstm",
    )(h2d, w, b, gain)


def _rglru_kernel(h_ref, w_ref, b_ref, cw_ref, cb_ref, wa_ref, ba_ref, wx_ref, bx_ref,
                  lam_ref, o_ref, xpad_ref, hbuf_ref, hcar_ref):
    seq = h_ref.shape[0]
    L = CHUNK
    R = BRANCH_W
    pad = SUBLANES
    xpad_ref[0:pad, :] = jnp.zeros((pad, R), F32)
    hcar_ref[...] = jnp.zeros_like(hcar_ref)

    groups = L // SUBLANES

    def step(i, carry):
        r0 = pl.multiple_of(i * L, L)
        z = _dot(h_ref[pl.ds(r0, L), :], w_ref[...]) + b_ref[...]
        xpad_ref[pad:pad + L, :] = z[:, :R]
        xp = xpad_ref[...].reshape(groups + 1, SUBLANES, R)
        xpad_ref[0:pad, :] = xp[groups]
        sub = lax.broadcasted_iota(jnp.int32, (groups, SUBLANES, R), 1)
        xc = cb_ref[...] + cw_ref[CONV_W - 1:CONV_W, :] * xp[1:]
        for k in range(1, CONV_W):
            rot = pltpu.roll(xp, k, 1)
            xc = xc + cw_ref[CONV_W - 1 - k:CONV_W - k, :] * jnp.where(sub >= k, rot[1:], rot[:-1])
        xc = xc.reshape(L, R)
        xcb = xc.astype(BF16)
        r2 = jnp.tanh(_dot(xcb, wa_ref[...]) + ba_ref[...]) + 1.0
        i2 = jnp.tanh(_dot(xcb, wx_ref[...]) + bx_ref[...]) + 1.0
        log_a = r2 * ((-0.5 * LRU_C) * _softplus(-lam_ref[...]))
        a = jnp.exp(log_a)
        th = jnp.tanh(log_a)
        u = jnp.exp(0.5 * jnp.log(-2.0 * th / (1.0 - th))) * (i2 * (0.5 * xc))
        a = a.reshape(groups, SUBLANES, R)
        u = u.reshape(groups, SUBLANES, R)
        for sft in (1, 2, 4):
            ok = sub >= sft
            u = jnp.where(ok, a * pltpu.roll(u, sft, 1) + u, u)
            a = jnp.where(ok, a * pltpu.roll(a, sft, 1), a)
        hprev = hcar_ref[...]
        for g in range(groups):
            rows = slice(g * SUBLANES, (g + 1) * SUBLANES)
            hg = a[g] * hprev + u[g]
            hbuf_ref[rows, :] = hg
            hprev = hg[SUBLANES - 1:SUBLANES, :]
        hcar_ref[...] = hprev
        o_ref[pl.ds(r0, L), :] = (hbuf_ref[...] * _gelu_tanh(z[:, R:2 * R])).astype(BF16)
        return carry

    lax.fori_loop(0, seq // L, step, 0)


def _rglru(h2d, w, layer, b, conv_w, conv_b, wa, ba, wx, bx, lam, batch, seq):
    d = h2d.shape[1]
    R = BRANCH_W
    return pl.pallas_call(
        _rglru_kernel,
        grid=(batch,),
        in_specs=[
            pl.BlockSpec((seq, d), lambda i: (i, 0)),
            _layer_spec((d, 2 * R), layer),
            _const_spec((1, 2 * R)),
            _const_spec((CONV_W, R)),
            _const_spec((1, R)),
            _const_spec((R, R)),
            _const_spec((1, R)),
            _const_spec((R, R)),
            _const_spec((1, R)),
            _const_spec((1, R)),
        ],
        out_specs=pl.BlockSpec((seq, R), lambda i: (i, 0)),
        out_shape=jax.ShapeDtypeStruct((batch * seq, R), BF16),
        scratch_shapes=[
            pltpu.VMEM((CHUNK + SUBLANES, R), F32),
            pltpu.VMEM((CHUNK, R), F32),
            pltpu.VMEM((1, R), F32),
        ],
        compiler_params=_params(("arbitrary",)),
        name="rglru",
    )(h2d, w, b, conv_w, conv_b, wa, ba, wx, bx, lam)


def _log_gamma(hd):
    return math.log1p(-(2.0 ** (-5.0 - hd)))


def _retention_kernel(h_ref, w_ref, b_ref, cos_ref, sin_ref, gain_ref, o_ref, state_ref,
                      decay_ref, z_ref):
    seq = h_ref.shape[0]
    L = CHUNK
    nc = seq // L
    W = HEADS * DH
    state_ref[...] = jnp.zeros_like(state_ref)
    row = lax.broadcasted_iota(jnp.int32, (L, L), 0)
    col = lax.broadcasted_iota(jnp.int32, (L, L), 1)
    rel = (row - col).astype(F32)
    for hd in range(HEADS):
        decay_ref[hd] = jnp.where(rel >= 0.0, jnp.exp(rel * _log_gamma(hd)), 0.0)

    def project(i):
        r0 = pl.multiple_of(i * L, L)
        z_ref[...] = _dot(h_ref[pl.ds(r0, L), :], w_ref[...]) + b_ref[...]

    def chunk(i, project_next):
        r0 = pl.multiple_of(i * L, L)
        z = z_ref[...]
        cosf = cos_ref[pl.ds(r0, L), :]
        sinf = sin_ref[pl.ds(r0, L), :]
        idx_col = lax.broadcasted_iota(jnp.int32, (L, 1), 0).astype(F32)
        idx_row = lax.broadcasted_iota(jnp.int32, (1, L), 1).astype(F32)
        heads = []
        for hd in range(HEADS):
            lo = hd * DH
            qf = z[:, lo:lo + DH]
            kf = z[:, W + lo:W + lo + DH]
            heads.append((
                (qf * cosf + pltpu.roll(qf, DH // 2, 1) * sinf).astype(BF16),
                ((kf * cosf + pltpu.roll(kf, DH // 2, 1) * sinf) * DH ** -0.5).T,
                z[:, 2 * W + lo:2 * W + lo + DH].astype(BF16),
                z[:, 3 * W + lo:3 * W + lo + DH]))
        if project_next:
            project(i + 1)
        for hd, (q, k_t, v, g) in enumerate(heads):
            lo = hd * DH
            lg = _log_gamma(hd)
            s = _dot(q, k_t.astype(BF16)) * decay_ref[hd]
            st = state_ref[hd]
            out = _dot(s.astype(BF16), v) + jnp.exp((idx_col + 1.0) * lg) * _dot(q, st.astype(BF16))
            k_w = (k_t * jnp.exp((L - 1.0 - idx_row) * lg)).astype(BF16)
            state_ref[hd] = math.exp(L * lg) * st + _dot(k_w, v)
            y = (g * jax.nn.sigmoid(g)) * _head_norm(out, gain_ref[:, lo:lo + DH])
            o_ref[pl.ds(r0, L), lo:lo + DH] = y.astype(BF16)

    project(0)
    for i in range(nc):
        chunk(i, project_next=i + 1 < nc)


def _retention(h2d, w, layer, b, cos_t, sin_t, gain, batch, seq):
    d = h2d.shape[1]
    n = w.shape[2]
    return pl.pallas_call(
        _retention_kernel,
        grid=(batch,),
        in_specs=[
            pl.BlockSpec((seq, d), lambda i: (i, 0)),
            _layer_spec((d, n), layer),
            _const_spec((1, n)),
            pl.BlockSpec((None, seq, DH), lambda i: (i, 0, 0)),
            pl.BlockSpec((None, seq, DH), lambda i: (i, 0, 0)),
            _const_spec((1, BRANCH_W)),
        ],
        out_specs=pl.BlockSpec((seq, BRANCH_W), lambda i: (i, 0)),
        out_shape=jax.ShapeDtypeStruct((batch * seq, BRANCH_W), BF16),
        scratch_shapes=[
            pltpu.VMEM((HEADS, DH, DH), F32),
            pltpu.VMEM((HEADS, CHUNK, CHUNK), F32),
            pltpu.VMEM((CHUNK, n), F32),
        ],
        compiler_params=_params(("arbitrary",)),
        name="retention",
    )(h2d, w, b, cos_t, sin_t, gain)


def _fox_kernel(h_ref, w_ref, b_ref, o_ref, kt_ref, v_ref, fc_ref, q_ref, acc_ref):
    seq = h_ref.shape[0]
    L = CHUNK
    W = HEADS * DH
    fc_ref[...] = jnp.zeros_like(fc_ref)
    tri = _prefix_sum_matrix(L, BF16)
    causal = _causal_mask(L)
    sub = lax.broadcasted_iota(jnp.int32, (SUBLANES, L), 0)
    lane = lax.broadcasted_iota(jnp.int32, (L, LANES), 1)
    q_tail = jnp.where(lane < 3, 1.0, 0.0).astype(BF16)
    k_pad = jnp.zeros((DH - SUBLANES, L), F32)
    ones_blk = _ones_column_block(L)

    for i in range(seq // L):
        r0 = i * L
        z = _dot(h_ref[r0:r0 + L, :], w_ref[...]) + b_ref[...]
        f_pre = z[:, 3 * W:3 * W + LANES].T[0:SUBLANES, :]
        cum_f = _prefix_sum_lanes(_log_sigmoid(f_pre), tri) + fc_ref[:, 0:1]
        fc_ref[...] = jnp.broadcast_to(cum_f[:, L - 1:L], (SUBLANES, LANES))
        f_pieces = _bf16_pieces(cum_f)
        row_max = []
        for hd in range(HEADS):
            lo = hd * DH
            p0, p1, p2 = (jnp.broadcast_to(-f[hd:hd + 1, :], (SUBLANES, L)) for f in f_pieces)
            bias_rows = jnp.where(sub == 0, p0, jnp.where(sub == 1, p1, jnp.where(sub == 2, p2, 0.0)))
            k_t = jnp.concatenate(
                [z[:, W + lo:W + lo + DH].T, bias_rows, k_pad], axis=0).astype(BF16)
            v_aug = jnp.concatenate(
                [z[:, 2 * W + lo:2 * W + lo + DH], ones_blk], axis=1).astype(BF16)
            kt_ref[hd, i] = k_t
            v_ref[hd, i] = v_aug
            q = jnp.concatenate(
                [(z[:, lo:lo + DH] * DH ** -0.5).astype(BF16), q_tail], axis=1)
            q_ref[hd] = q
            s = jnp.where(causal, _dot(q, k_t), -jnp.inf)
            m = jnp.max(s, axis=1, keepdims=True)
            acc_ref[hd] = _dot(jnp.exp(s - m).astype(BF16), v_aug)
            row_max.append(m)

        for j in range(i):
            for hd in range(HEADS):
                s = _dot(q_ref[hd], kt_ref[hd, j])
                m_new = jnp.maximum(row_max[hd], jnp.max(s, axis=1, keepdims=True))
                acc_ref[hd] = (jnp.exp(row_max[hd] - m_new) * acc_ref[hd]
                               + _dot(jnp.exp(s - m_new).astype(BF16), v_ref[hd, j]))
                row_max[hd] = m_new
        for hd in range(HEADS):
            acc = acc_ref[hd]
            o_ref[r0:r0 + L, hd * DH:(hd + 1) * DH] = (
                acc[:, :DH] / acc[:, DH:DH + 1]).astype(BF16)


def _fox(h2d, w, layer, b, batch, seq):
    d = h2d.shape[1]
    n = w.shape[2]
    nc = seq // CHUNK
    return pl.pallas_call(
        _fox_kernel,
        grid=(batch,),
        in_specs=[
            pl.BlockSpec((seq, d), lambda i: (i, 0)),
            _layer_spec((d, n), layer),
            _const_spec((1, n)),
        ],
        out_specs=pl.BlockSpec((seq, BRANCH_W), lambda i: (i, 0)),
        out_shape=jax.ShapeDtypeStruct((batch * seq, BRANCH_W), BF16),
        scratch_shapes=[
            pltpu.VMEM((HEADS, nc, 2 * DH, CHUNK), BF16),
            pltpu.VMEM((HEADS, nc, CHUNK, 2 * DH), BF16),
            pltpu.VMEM((SUBLANES, LANES), F32),
            pltpu.VMEM((HEADS, CHUNK, 2 * DH), BF16),
            pltpu.VMEM((HEADS, CHUNK, 2 * DH), F32),
        ],
        compiler_params=_params(("arbitrary",)),
        name="forgetting_attention",
    )(h2d, w, b)


def _merge_kernel(h_ref, ym_ref, yl_ref, yr_ref, yf_ref, x_ref, mod_ref, wg_ref, bg_ref,
                  wbr_ref, wo_ref, bo_ref, lng_ref, lnb_ref, o_ref, hn_ref, *, alpha):
    tm, d = x_ref.shape
    for r0 in range(0, tm, tm // MERGE_SPLIT):
        rows = slice(r0, r0 + tm // MERGE_SPLIT)
        h = h_ref[rows, :]
        merged = None
        for n, y_ref in enumerate((ym_ref, yl_ref, yr_ref, yf_ref)):
            gate = jax.nn.sigmoid(
                _dot(h, wg_ref[:, n * d:(n + 1) * d]) + bg_ref[:, n * d:(n + 1) * d])
            term = gate * _dot(y_ref[rows, :], wbr_ref[n])
            merged = term if merged is None else merged + term
        y = _dot(merged.astype(BF16), wo_ref[...]) + bo_ref[...]
        res = alpha * x_ref[rows, :] + (1.0 + mod_ref[2:3, :]) * y
        x_new = _layernorm(res) * lng_ref[...] + lnb_ref[...]
        o_ref[rows, :] = x_new
        hn_ref[rows, :] = _ln_modulated(x_new, mod_ref, shift_row=3, scale_row=4)


def _merge(h2d, ys, x2d, mod_l, wg, layer, bg, wbr, wo, bo, lng, lnb, seq, alpha):
    tokens, d = x2d.shape
    tm = ROW_TILE
    tiles_per_seq = seq // tm
    row = lambda i: (i, 0)
    return pl.pallas_call(
        functools.partial(_merge_kernel, alpha=alpha),
        grid=(tokens // tm,),
        in_specs=[
            pl.BlockSpec((tm, d), row),
            pl.BlockSpec((tm, BRANCH_W), row),
            pl.BlockSpec((tm, BRANCH_W), row),
            pl.BlockSpec((tm, BRANCH_W), row),
            pl.BlockSpec((tm, BRANCH_W), row),
            pl.BlockSpec((tm, d), row),
            pl.BlockSpec((None, 6, d), lambda i: (i // tiles_per_seq, 0, 0)),
            _layer_spec((d, N_BRANCH * d), layer),
            _const_spec((1, N_BRANCH * d)),
            _const_spec((N_BRANCH, BRANCH_W, d)),
            _const_spec((d, d)),
            _const_spec((1, d)),
            _const_spec((1, d)),
            _const_spec((1, d)),
        ],
        out_specs=[pl.BlockSpec((tm, d), row), pl.BlockSpec((tm, d), row)],
        out_shape=[jax.ShapeDtypeStruct((tokens, d), F32),
                   jax.ShapeDtypeStruct((tokens, d), BF16)],
        compiler_params=_params(("parallel",), VMEM_LIMIT_BYTES_WIDE),
        name="merge_out_proj",
    )(h2d, *ys, x2d, mod_l, wg, bg, wbr, wo, bo, lng, lnb)


def _mlp_kernel(h_ref, x_ref, mod_ref, w1_ref, b1_ref, w2_ref, b2_ref, lng_ref, lnb_ref,
                *rest, alpha, ff_tile, emit_next):
    d_ff = w1_ref.shape[1]
    tm = x_ref.shape[0]
    if emit_next:
        next_mod_ref, o_ref, hn_ref = rest
    else:
        (o_ref,) = rest
    for r0 in range(0, tm, tm // MERGE_SPLIT):
        rows = slice(r0, r0 + tm // MERGE_SPLIT)
        h = h_ref[rows, :]
        acc = None
        for c in range(d_ff // ff_tile):
            cols = slice(c * ff_tile, (c + 1) * ff_tile)
            a = jnp.maximum(_dot(h, w1_ref[:, cols]) + b1_ref[:, cols], 0.0)
            part = _dot((a * a).astype(BF16), w2_ref[cols, :])
            acc = part if acc is None else acc + part
        y = acc + b2_ref[...]
        res = alpha * x_ref[rows, :] + (1.0 + mod_ref[5:6, :]) * y
        x_new = _layernorm(res) * lng_ref[...] + lnb_ref[...]
        o_ref[rows, :] = x_new
        if emit_next:
            hn_ref[rows, :] = _ln_modulated(x_new, next_mod_ref, shift_row=0, scale_row=1)


def _mlp(h2d, x2d, mod_l, next_mod, w1, b1, w2, b2, lng, lnb, seq, alpha):
    tokens, d = x2d.shape
    d_ff = w1.shape[1]
    tm = ROW_TILE
    tiles_per_seq = seq // tm
    row = lambda i: (i, 0)
    mod_spec = pl.BlockSpec((None, 6, d), lambda i: (i // tiles_per_seq, 0, 0))
    emit_next = next_mod is not None
    in_specs = [
        pl.BlockSpec((tm, d), row),
        pl.BlockSpec((tm, d), row),
        mod_spec,
        _const_spec((d, d_ff)),
        _const_spec((1, d_ff)),
        _const_spec((d_ff, d)),
        _const_spec((1, d)),
        _const_spec((1, d)),
        _const_spec((1, d)),
    ]
    args = [h2d, x2d, mod_l, w1, b1, w2, b2, lng, lnb]
    out_specs = [pl.BlockSpec((tm, d), row)]
    out_shape = [jax.ShapeDtypeStruct((tokens, d), F32)]
    if emit_next:
        in_specs.append(mod_spec)
        args.append(next_mod)
        out_specs.append(pl.BlockSpec((tm, d), row))
        out_shape.append(jax.ShapeDtypeStruct((tokens, d), BF16))
    outs = pl.pallas_call(
        functools.partial(_mlp_kernel, alpha=alpha, ff_tile=d, emit_next=emit_next),
        grid=(tokens // tm,),
        in_specs=in_specs,
        out_specs=out_specs,
        out_shape=out_shape,
        compiler_params=_params(("parallel",), VMEM_LIMIT_BYTES_WIDE),
        name="sq_relu_mlp",
    )(*args)
    return (outs[0], outs[1]) if emit_next else (outs[0], None)


_PACK_GROUPS = {
    "mlstm": (("m_q", "m_k", "m_v", "m_o"), ("m_i", "m_f")),
    "rglru": (("l_x", "l_g"), ()),
    "retention": (("r_q", "r_k", "r_v", "r_g"), ()),
    "fox": (("f_q", "f_k", "f_v"), ("f_f",)),
    "gate": (("gate",), ()),
}
PACK_COLS = 512


def _pack_kernel(w_ref, o_ref, *, valid):
    depth, d = w_ref.shape[1], w_ref.shape[2]
    col = pl.program_id(0) * PACK_COLS + lax.broadcasted_iota(jnp.int32, (d, PACK_COLS), 1)
    for l in range(depth):
        o_ref[l] = jnp.where(col < valid, w_ref[:, l, :].T, 0.0).astype(BF16)


def _pack_group(w_cols, seg, names, gate_names):
    n_in, depth, d = w_cols.shape
    start = seg[names[0]][0]
    main = seg[names[-1]][1] - start
    assert main % LANES == 0
    gate_w = sum(seg[n][1] - seg[n][0] for n in gate_names)
    assert not gate_names or seg[gate_names[0]][0] == start + main
    width = main + (LANES if gate_names else 0)
    n_blocks = pl.cdiv(width, PACK_COLS)
    assert start + n_blocks * PACK_COLS <= n_in
    return pl.pallas_call(
        functools.partial(_pack_kernel, valid=main + gate_w),
        grid=(n_blocks,),
        in_specs=[pl.BlockSpec((pl.Element(PACK_COLS), pl.Element(depth), pl.Element(d)),
                               lambda j: (start + j * PACK_COLS, 0, 0))],
        out_specs=pl.BlockSpec((depth, d, PACK_COLS), lambda j: (0, 0, j)),
        out_shape=jax.ShapeDtypeStruct((depth, d, width), BF16),
        compiler_params=_params(("parallel",)),
        name="pack_in_proj",
    )(w_cols)


def _pack_weights(w_in, seg):
    w_cols = jnp.transpose(w_in, (2, 0, 1))
    return {k: _pack_group(w_cols, seg, *grp) for k, grp in _PACK_GROUPS.items()}


def _pack_bias(b_in_l, seg, key):
    names, gate_names = _PACK_GROUPS[key]
    bs = [b_in_l[seg[n][0]:seg[n][1]] for n in names]
    if gate_names:
        gb = jnp.concatenate([b_in_l[seg[n][0]:seg[n][1]] for n in gate_names])
        bs.append(jnp.pad(gb, (0, LANES - gb.shape[0])))
    return jnp.concatenate(bs)[None, :]


def _block_diag(w):
    nb, bd, _ = w.shape
    eye = jnp.eye(nb, dtype=w.dtype)
    return (eye[:, None, :, None] * w[:, :, None, :]).reshape(nb * bd, nb * bd)


def kernel(x, c, positions, w_ada, b_ada, w_in, b_in, m_norm, conv_w, conv_b, lru_wa, lru_ba,
           lru_wx, lru_bx, lru_lam, r_norm, w_br, w_out, b_out, ln1_g, ln1_b, w_ff1, b_ff1,
           w_ff2, b_ff2, ln2_g, ln2_b):
    batch, seq, d = x.shape
    depth = w_ada.shape[0]
    assert seq % CHUNK == 0 and seq % ROW_TILE == 0 and d % LANES == 0
    alpha = (2 * depth) ** 0.25
    seg = _segment_slices(d)

    mod = _modulation(c, w_ada, b_ada).reshape(depth, batch, 6, d)
    cos_t, sin_t = _rope_tables(positions)
    x2d = x.reshape(batch * seq, d)
    w = _pack_weights(w_in, seg)
    row = lambda v: v[None, :]
    h1 = _ln_modulate(x2d, mod[0], seq, shift_row=0, scale_row=1)
    for l in range(depth):
        bias = lambda key: _pack_bias(b_in[l], seg, key)
        y_m = _mlstm(h1, w["mlstm"], l, bias("mlstm"), row(m_norm[l]), batch, seq)
        y_l = _rglru(h1, w["rglru"], l, bias("rglru"), conv_w[l], row(conv_b[l]),
                     (0.5 * _block_diag(lru_wa[l])).astype(BF16), row(0.5 * lru_ba[l]),
                     (0.5 * _block_diag(lru_wx[l])).astype(BF16), row(0.5 * lru_bx[l]),
                     row(lru_lam[l]), batch, seq)
        y_r = _retention(h1, w["retention"], l, bias("retention"), cos_t, sin_t,
                         row(r_norm[l]), batch, seq)
        y_f = _fox(h1, w["fox"], l, bias("fox"), batch, seq)
        x2d, h2 = _merge(h1, (y_m, y_l, y_r, y_f), x2d, mod[l], w["gate"], l, bias("gate"),
                         w_br[l].astype(BF16), w_out[l].astype(BF16), row(b_out[l]),
                         row(ln1_g[l]), row(ln1_b[l]), seq, alpha)
        next_mod = mod[l + 1] if l + 1 < depth else None
        x2d, h1 = _mlp(h2, x2d, mod[l], next_mod, w_ff1[l].astype(BF16), row(b_ff1[l]),
                       w_ff2[l].astype(BF16), row(b_ff2[l]), row(ln2_g[l]), row(ln2_b[l]),
                       seq, alpha)
    return x2d.reshape(batch, seq, d)
```

```python
import functools
import math

import jax
import jax.numpy as jnp
from jax import lax
from jax.experimental import pallas as pl
from jax.experimental.pallas import tpu as pltpu

F32 = jnp.float32
BF16 = jnp.bfloat16

LN_EPS = 1e-5
HEADS = 4
DH = 128
BRANCH_W = HEADS * DH
LRU_BLOCKS = 8
LRU_C = 8.0
CONV_W = 4
ROPE_BASE = 10000.0
N_BRANCH = 4

LANES = 128
SUBLANES = 8
CHUNK = 256
ROW_TILE = 1024
MERGE_SPLIT = 4
VMEM_LIMIT_BYTES = 48 * 1024 * 1024
VMEM_LIMIT_BYTES_WIDE = 58 * 1024 * 1024

_SEGMENTS = (
    ("m_q", BRANCH_W), ("m_k", BRANCH_W), ("m_v", BRANCH_W), ("m_o", BRANCH_W),
    ("m_i", HEADS), ("m_f", HEADS),
    ("l_x", BRANCH_W), ("l_g", BRANCH_W),
    ("r_q", BRANCH_W), ("r_k", BRANCH_W), ("r_v", BRANCH_W), ("r_g", BRANCH_W),
    ("f_q", BRANCH_W), ("f_k", BRANCH_W), ("f_v", BRANCH_W), ("f_f", HEADS),
    ("gate", None),
)


def _segment_slices(d_model):
    out, acc = {}, 0
    for name, width in _SEGMENTS:
        width = N_BRANCH * d_model if width is None else width
        out[name] = (acc, acc + width)
        acc += width
    return out


def _dot(a, b):
    return jnp.dot(a, b, preferred_element_type=F32)


def _dot_f32(a, b):
    return jnp.dot(a, b, preferred_element_type=F32, precision=lax.Precision.HIGHEST)


def _layernorm(x):
    mu = jnp.mean(x, axis=-1, keepdims=True)
    xc = x - mu
    var = jnp.mean(xc * xc, axis=-1, keepdims=True)
    return xc * lax.rsqrt(var + LN_EPS)


def _log_sigmoid(x):
    return -(jnp.maximum(-x, 0.0) + jnp.log1p(jnp.exp(-jnp.abs(x))))


def _softplus(x):
    return jnp.maximum(x, 0.0) + jnp.log1p(jnp.exp(-jnp.abs(x)))


def _gelu_tanh(x):
    return x * (0.5 * (1.0 + jnp.tanh(math.sqrt(2.0 / math.pi) * (x + 0.044715 * (x * x * x)))))


def _causal_mask(n):
    row = lax.broadcasted_iota(jnp.int32, (n, n), 0)
    col = lax.broadcasted_iota(jnp.int32, (n, n), 1)
    return row >= col


def _prefix_sum_matrix(n, dtype):
    row = lax.broadcasted_iota(jnp.int32, (n, n), 0)
    col = lax.broadcasted_iota(jnp.int32, (n, n), 1)
    return jnp.where(row <= col, 1.0, 0.0).astype(dtype)


def _bf16_pieces(x):
    hi = x.astype(BF16).astype(F32)
    mid = (x - hi).astype(BF16).astype(F32)
    lo = (x - hi - mid).astype(BF16).astype(F32)
    return hi, mid, lo


def _prefix_sum_lanes(x, tri):
    rows = x.shape[0]
    stacked = jnp.concatenate(_bf16_pieces(x) + (jnp.zeros_like(x),), axis=0).astype(BF16)
    y = _dot(stacked, tri)
    return y[0:rows] + y[rows:2 * rows] + y[2 * rows:3 * rows]


def _ones_column_block(n):
    lane = lax.broadcasted_iota(jnp.int32, (n, LANES), 1)
    return jnp.where(lane == 0, 1.0, 0.0).astype(F32)


def _head_norm(y, gain):
    mu = jnp.mean(y, axis=-1, keepdims=True)
    yc = y - mu
    var = jnp.mean(yc * yc, axis=-1, keepdims=True)
    return yc * lax.rsqrt(var + LN_EPS) * gain


def _const_spec(shape):
    zeros = (0,) * len(shape)
    return pl.BlockSpec(shape, lambda *_: zeros, pipeline_mode=pl.Buffered(1))


def _layer_spec(shape, layer):
    idx = (layer,) + (0,) * len(shape)
    return pl.BlockSpec((None,) + tuple(shape), lambda *_: idx, pipeline_mode=pl.Buffered(1))


def _params(semantics, vmem_limit=VMEM_LIMIT_BYTES):
    return pltpu.CompilerParams(dimension_semantics=semantics, vmem_limit_bytes=vmem_limit)


def _mod_kernel(c_ref, w_ref, b_ref, o_ref):
    c = c_ref[...]
    cond = c * jax.nn.sigmoid(c)
    o_ref[...] = _dot_f32(cond, w_ref[...]) + b_ref[...]


def _modulation(c, w_ada, b_ada):
    depth, d, six_d = w_ada.shape
    batch = c.shape[0]
    n_tiles = six_d // d
    return pl.pallas_call(
        _mod_kernel,
        grid=(depth, n_tiles),
        in_specs=[
            pl.BlockSpec((batch, d), lambda l, j: (0, 0)),
            pl.BlockSpec((None, d, d), lambda l, j: (l, 0, j)),
            pl.BlockSpec((None, 1, d), lambda l, j: (l, 0, j)),
        ],
        out_specs=pl.BlockSpec((None, batch, d), lambda l, j: (l, 0, j)),
        out_shape=jax.ShapeDtypeStruct((depth, batch, six_d), F32),
        compiler_params=_params(("parallel", "parallel")),
        name="adaln_modulation",
    )(c, w_ada, b_ada.reshape(depth, 1, six_d))


def _rope_kernel(pos_ref, invf_ref, cos_ref, sin_ref):
    ang = invf_ref[...] * pos_ref[...].astype(F32)
    c = jnp.cos(ang)
    s = jnp.sin(ang)
    cos_ref[...] = jnp.concatenate([c, c], axis=0).T
    sin_ref[...] = jnp.concatenate([-s, s], axis=0).T


def _rope_tables(positions):
    batch, seq = positions.shape
    inv_freq = ROPE_BASE ** (-jnp.arange(0, DH, 2, dtype=F32) / DH)
    out = jax.ShapeDtypeStruct((batch, seq, DH), F32)
    return pl.pallas_call(
        _rope_kernel,
        grid=(batch,),
        in_specs=[
            pl.BlockSpec((None, 1, seq), lambda b: (b, 0, 0)),
            pl.BlockSpec((DH // 2, 1), lambda b: (0, 0)),
        ],
        out_specs=[pl.BlockSpec((None, seq, DH), lambda b: (b, 0, 0))] * 2,
        out_shape=[out, out],
        compiler_params=_params(("parallel",)),
        name="rope_tables",
    )(positions.reshape(batch, 1, seq), inv_freq.reshape(DH // 2, 1))


def _ln_modulated(x, mod_ref, shift_row, scale_row):
    scale = mod_ref[scale_row:scale_row + 1, :]
    shift = mod_ref[shift_row:shift_row + 1, :]
    return (_layernorm(x) * (1.0 + scale) + shift).astype(BF16)


def _ln_mod_kernel(x_ref, mod_ref, o_ref, *, shift_row, scale_row):
    o_ref[...] = _ln_modulated(x_ref[...], mod_ref, shift_row, scale_row)


def _ln_modulate(x2d, mod_l, seq, shift_row, scale_row):
    tokens, d = x2d.shape
    tm = ROW_TILE
    tiles_per_seq = seq // tm
    return pl.pallas_call(
        functools.partial(_ln_mod_kernel, shift_row=shift_row, scale_row=scale_row),
        grid=(tokens // tm,),
        in_specs=[
            pl.BlockSpec((tm, d), lambda i: (i, 0)),
            pl.BlockSpec((None, 6, d), lambda i: (i // tiles_per_seq, 0, 0)),
        ],
        out_specs=pl.BlockSpec((tm, d), lambda i: (i, 0)),
        out_shape=jax.ShapeDtypeStruct((tokens, d), BF16),
        compiler_params=_params(("parallel",)),
        name="ln_modulate",
    )(x2d, mod_l)


def _mlstm_kernel(h_ref, w_ref, b_ref, gain_ref, o_ref, state_ref, mc_ref, bc_ref, z_ref):
    seq = h_ref.shape[0]
    L = CHUNK
    nc = seq // L
    qkvo = HEADS * DH
    state_ref[...] = jnp.zeros_like(state_ref)
    mc_ref[...] = jnp.zeros_like(mc_ref)
    bc_ref[...] = jnp.zeros_like(bc_ref)
    tri = _prefix_sum_matrix(L, F32)

    def project(i):
        r0 = pl.multiple_of(i * L, L)
        z_ref[...] = _dot(h_ref[pl.ds(r0, L), :], w_ref[...]) + b_ref[...]

    def chunk(i, project_next):
        r0 = pl.multiple_of(i * L, L)
        z = z_ref[...]
        causal = _causal_mask(L)
        gates = z[:, 4 * qkvo:4 * qkvo + LANES].T[0:2 * HEADS, :]
        log_f = _log_sigmoid(gates)
        cum_f = _dot_f32(log_f, tri)
        ones_blk = _ones_column_block(L)
        heads = []
        for hd in range(HEADS):
            lo = hd * DH
            heads.append((
                z[:, lo:lo + DH].astype(BF16),
                (z[:, qkvo + lo:qkvo + lo + DH] * DH ** -0.5).T,
                jnp.concatenate(
                    [z[:, 2 * qkvo + lo:2 * qkvo + lo + DH], ones_blk], axis=1).astype(BF16),
                jax.nn.sigmoid(z[:, 3 * qkvo + lo:3 * qkvo + lo + DH])))
        if project_next:
            project(i + 1)
        for hd, (q, k_t, v_aug, o_gate) in enumerate(heads):
            lo = hd * DH
            mc = mc_ref[hd][:, 0:1]
            bc = bc_ref[hd][:, 0:1]
            lf_row = log_f[HEADS + hd:HEADS + hd + 1, :]
            b_row = cum_f[HEADS + hd:HEADS + hd + 1, :] + bc
            c_row = gates[hd:hd + 1, :] - b_row
            d0 = jnp.where(causal, c_row, -jnp.inf)
            m_t = jnp.maximum(jnp.max(d0, axis=1, keepdims=True), mc)
            p = jnp.exp(d0 - m_t)
            b_col = jnp.sum(jnp.where(causal, lf_row, 0.0), axis=1, keepdims=True) + bc
            s = _dot(q, k_t.astype(BF16))
            intra = _dot((s * p).astype(BF16), v_aug)
            st = state_ref[hd]
            inter = _dot(q, st.astype(BF16))
            tot = jnp.exp(mc - m_t) * inter + intra
            den = tot[:, DH:DH + 1]
            hh = tot[:, :DH] / jnp.maximum(jnp.abs(den), jnp.exp(-(b_col + m_t)))
            y = _head_norm(o_gate * hh, gain_ref[:, lo:lo + DH])
            o_ref[pl.ds(r0, L), lo:lo + DH] = y.astype(BF16)
            mc_new = m_t[L - 1:L, :]
            k_w = (k_t * jnp.exp(c_row - mc_new)).astype(BF16)
            state_ref[hd] = jnp.exp(mc - mc_new) * st + _dot(k_w, v_aug)
            mc_ref[hd] = jnp.broadcast_to(mc_new, (1, LANES))
            bc_ref[hd] = jnp.broadcast_to(b_row[:, L - 1:L], (1, LANES))

    project(0)
    for i in range(nc):
        chunk(i, project_next=i + 1 < nc)


def _mlstm(h2d, w, layer, b, gain, batch, seq):
    d = h2d.shape[1]
    n = w.shape[2]
    return pl.pallas_call(
        _mlstm_kernel,
        grid=(batch,),
        in_specs=[
            pl.BlockSpec((seq, d), lambda i: (i, 0)),
            _layer_spec((d, n), layer),
            _const_spec((1, n)),
            _const_spec((1, BRANCH_W)),
        ],
        out_specs=pl.BlockSpec((seq, BRANCH_W), lambda i: (i, 0)),
        out_shape=jax.ShapeDtypeStruct((batch * seq, BRANCH_W), BF16),
        scratch_shapes=[
            pltpu.VMEM((HEADS, DH, 2 * DH), F32),
            pltpu.VMEM((HEADS, 1, LANES), F32),
            pltpu.VMEM((HEADS, 1, LANES), F32),
            pltpu.VMEM((CHUNK, n), F32),
        ],
        compiler_params=_params(("arbitrary",)),
        name="mlstm",
    )(h2d, w, b, gain)


def _rglru_kernel(h_ref, w_ref, b_ref, cw_ref, cb_ref, wa_ref, ba_ref, wx_ref, bx_ref,
                  lam_ref, o_ref, xpad_ref, hbuf_ref, hcar_ref):
    seq = h_ref.shape[0]
    L = CHUNK
    R = BRANCH_W
    pad = SUBLANES
    xpad_ref[0:pad, :] = jnp.zeros((pad, R), F32)
    hcar_ref[...] = jnp.zeros_like(hcar_ref)

    groups = L // SUBLANES

    def step(i, carry):
        r0 = pl.multiple_of(i * L, L)
        z = _dot(h_ref[pl.ds(r0, L), :], w_ref[...]) + b_ref[...]
        xpad_ref[pad:pad + L, :] = z[:, :R]
        xp = xpad_ref[...].reshape(groups + 1, SUBLANES, R)
        xpad_ref[0:pad, :] = xp[groups]
        sub = lax.broadcasted_iota(jnp.int32, (groups, SUBLANES, R), 1)
        xc = cb_ref[...] + cw_ref[CONV_W - 1:CONV_W, :] * xp[1:]
        for k in range(1, CONV_W):
            rot = pltpu.roll(xp, k, 1)
            xc = xc + cw_ref[CONV_W - 1 - k:CONV_W - k, :] * jnp.where(sub >= k, rot[1:], rot[:-1])
        xc = xc.reshape(L, R)
        xcb = xc.astype(BF16)
        r2 = jnp.tanh(_dot(xcb, wa_ref[...]) + ba_ref[...]) + 1.0
        i2 = jnp.tanh(_dot(xcb, wx_ref[...]) + bx_ref[...]) + 1.0
        log_a = r2 * ((-0.5 * LRU_C) * _softplus(-lam_ref[...]))
        a = jnp.exp(log_a)
        th = jnp.tanh(log_a)
        u = jnp.exp(0.5 * jnp.log(-2.0 * th / (1.0 - th))) * (i2 * (0.5 * xc))
        a = a.reshape(groups, SUBLANES, R)
        u = u.reshape(groups, SUBLANES, R)
        for sft in (1, 2, 4):
            ok = sub >= sft
            u = jnp.where(ok, a * pltpu.roll(u, sft, 1) + u, u)
            a = jnp.where(ok, a * pltpu.roll(a, sft, 1), a)
        hprev = hcar_ref[...]
        for g in range(groups):
            rows = slice(g * SUBLANES, (g + 1) * SUBLANES)
            hg = a[g] * hprev + u[g]
            hbuf_ref[rows, :] = hg
            hprev = hg[SUBLANES - 1:SUBLANES, :]
        hcar_ref[...] = hprev
        o_ref[pl.ds(r0, L), :] = (hbuf_ref[...] * _gelu_tanh(z[:, R:2 * R])).astype(BF16)
        return carry

    lax.fori_loop(0, seq // L, step, 0)


def _rglru(h2d, w, layer, b, conv_w, conv_b, wa, ba, wx, bx, lam, batch, seq):
    d = h2d.shape[1]
    R = BRANCH_W
    return pl.pallas_call(
        _rglru_kernel,
        grid=(batch,),
        in_specs=[
            pl.BlockSpec((seq, d), lambda i: (i, 0)),
            _layer_spec((d, 2 * R), layer),
            _const_spec((1, 2 * R)),
            _const_spec((CONV_W, R)),
            _const_spec((1, R)),
            _const_spec((R, R)),
            _const_spec((1, R)),
            _const_spec((R, R)),
            _const_spec((1, R)),
            _const_spec((1, R)),
        ],
        out_specs=pl.BlockSpec((seq, R), lambda i: (i, 0)),
        out_shape=jax.ShapeDtypeStruct((batch * seq, R), BF16),
        scratch_shapes=[
            pltpu.VMEM((CHUNK + SUBLANES, R), F32),
            pltpu.VMEM((CHUNK, R), F32),
            pltpu.VMEM((1, R), F32),
        ],
        compiler_params=_params(("arbitrary",)),
        name="rglru",
    )(h2d, w, b, conv_w, conv_b, wa, ba, wx, bx, lam)


def _log_gamma(hd):
    return math.log1p(-(2.0 ** (-5.0 - hd)))


def _retention_kernel(h_ref, w_ref, b_ref, cos_ref, sin_ref, gain_ref, o_ref, state_ref,
                      decay_ref, z_ref):
    seq = h_ref.shape[0]
    L = CHUNK
    nc = seq // L
    W = HEADS * DH
    state_ref[...] = jnp.zeros_like(state_ref)
    row = lax.broadcasted_iota(jnp.int32, (L, L), 0)
    col = lax.broadcasted_iota(jnp.int32, (L, L), 1)
    rel = (row - col).astype(F32)
    for hd in range(HEADS):
        decay_ref[hd] = jnp.where(rel >= 0.0, jnp.exp(rel * _log_gamma(hd)), 0.0)

    def project(i):
        r0 = pl.multiple_of(i * L, L)
        z_ref[...] = _dot(h_ref[pl.ds(r0, L), :], w_ref[...]) + b_ref[...]

    def chunk(i, project_next):
        r0 = pl.multiple_of(i * L, L)
        z = z_ref[...]
        cosf = cos_ref[pl.ds(r0, L), :]
        sinf = sin_ref[pl.ds(r0, L), :]
        idx_col = lax.broadcasted_iota(jnp.int32, (L, 1), 0).astype(F32)
        idx_row = lax.broadcasted_iota(jnp.int32, (1, L), 1).astype(F32)
        heads = []
        for hd in range(HEADS):
            lo = hd * DH
            qf = z[:, lo:lo + DH]
            kf = z[:, W + lo:W + lo + DH]
            heads.append((
                (qf * cosf + pltpu.roll(qf, DH // 2, 1) * sinf).astype(BF16),
                ((kf * cosf + pltpu.roll(kf, DH // 2, 1) * sinf) * DH ** -0.5).T,
                z[:, 2 * W + lo:2 * W + lo + DH].astype(BF16),
                z[:, 3 * W + lo:3 * W + lo + DH]))
        if project_next:
            project(i + 1)
        for hd, (q, k_t, v, g) in enumerate(heads):
            lo = hd * DH
            lg = _log_gamma(hd)
            s = _dot(q, k_t.astype(BF16)) * decay_ref[hd]
            st = state_ref[hd]
            out = _dot(s.astype(BF16), v) + jnp.exp((idx_col + 1.0) * lg) * _dot(q, st.astype(BF16))
            k_w = (k_t * jnp.exp((L - 1.0 - idx_row) * lg)).astype(BF16)
            state_ref[hd] = math.exp(L * lg) * st + _dot(k_w, v)
            y = (g * jax.nn.sigmoid(g)) * _head_norm(out, gain_ref[:, lo:lo + DH])
            o_ref[pl.ds(r0, L), lo:lo + DH] = y.astype(BF16)

    project(0)
    for i in range(nc):
        chunk(i, project_next=i + 1 < nc)


def _retention(h2d, w, layer, b, cos_t, sin_t, gain, batch, seq):
    d = h2d.shape[1]
    n = w.shape[2]
    return pl.pallas_call(
        _retention_kernel,
        grid=(batch,),
        in_specs=[
            pl.BlockSpec((seq, d), lambda i: (i, 0)),
            _layer_spec((d, n), layer),
            _const_spec((1, n)),
            pl.BlockSpec((None, seq, DH), lambda i: (i, 0, 0)),
            pl.BlockSpec((None, seq, DH), lambda i: (i, 0, 0)),
            _const_spec((1, BRANCH_W)),
        ],
        out_specs=pl.BlockSpec((seq, BRANCH_W), lambda i: (i, 0)),
        out_shape=jax.ShapeDtypeStruct((batch * seq, BRANCH_W), BF16),
        scratch_shapes=[
            pltpu.VMEM((HEADS, DH, DH), F32),
            pltpu.VMEM((HEADS, CHUNK, CHUNK), F32),
            pltpu.VMEM((CHUNK, n), F32),
        ],
        compiler_params=_params(("arbitrary",)),
        name="retention",
    )(h2d, w, b, cos_t, sin_t, gain)


def _fox_kernel(h_ref, w_ref, b_ref, o_ref, kt_ref, v_ref, fc_ref, q_ref, acc_ref, z_ref):
    seq = h_ref.shape[0]
    L = CHUNK
    W = HEADS * DH
    fc_ref[...] = jnp.zeros_like(fc_ref)
    tri = _prefix_sum_matrix(L, BF16)
    causal = _causal_mask(L)
    sub = lax.broadcasted_iota(jnp.int32, (SUBLANES, L), 0)
    lane = lax.broadcasted_iota(jnp.int32, (L, LANES), 1)
    q_tail = jnp.where(lane < 3, 1.0, 0.0).astype(BF16)
    k_pad = jnp.zeros((DH - SUBLANES, L), F32)
    ones_blk = _ones_column_block(L)

    def project(i):
        z_ref[...] = _dot(h_ref[i * L:(i + 1) * L, :], w_ref[...]) + b_ref[...]

    project(0)
    for i in range(seq // L):
        r0 = i * L
        z = z_ref[...]
        f_pre = z[:, 3 * W:3 * W + LANES].T[0:SUBLANES, :]
        cum_f = _prefix_sum_lanes(_log_sigmoid(f_pre), tri) + fc_ref[:, 0:1]
        fc_ref[...] = jnp.broadcast_to(cum_f[:, L - 1:L], (SUBLANES, LANES))
        f_pieces = _bf16_pieces(cum_f)
        row_max = []
        for hd in range(HEADS):
            lo = hd * DH
            p0, p1, p2 = (jnp.broadcast_to(-f[hd:hd + 1, :], (SUBLANES, L)) for f in f_pieces)
            bias_rows = jnp.where(sub == 0, p0, jnp.where(sub == 1, p1, jnp.where(sub == 2, p2, 0.0)))
            k_t = jnp.concatenate(
                [z[:, W + lo:W + lo + DH].T, bias_rows, k_pad], axis=0).astype(BF16)
            v_aug = jnp.concatenate(
                [z[:, 2 * W + lo:2 * W + lo + DH], ones_blk], axis=1).astype(BF16)
            kt_ref[hd, i] = k_t
            v_ref[hd, i] = v_aug
            q = jnp.concatenate(
                [(z[:, lo:lo + DH] * DH ** -0.5).astype(BF16), q_tail], axis=1)
            q_ref[hd] = q
            s = jnp.where(causal, _dot(q, k_t), -jnp.inf)
            m = jnp.max(s, axis=1, keepdims=True)
            acc_ref[hd] = _dot(jnp.exp(s - m).astype(BF16), v_aug)
            row_max.append(m)

        for j in range(i):
            for hd in range(HEADS):
                s = _dot(q_ref[hd], kt_ref[hd, j])
                m_new = jnp.maximum(row_max[hd], jnp.max(s, axis=1, keepdims=True))
                acc_ref[hd] = (jnp.exp(row_max[hd] - m_new) * acc_ref[hd]
                               + _dot(jnp.exp(s - m_new).astype(BF16), v_ref[hd, j]))
                row_max[hd] = m_new
        if i + 1 < seq // L:
            project(i + 1)
        for hd in range(HEADS):
            acc = acc_ref[hd]
            o_ref[r0:r0 + L, hd * DH:(hd + 1) * DH] = (
                acc[:, :DH] / acc[:, DH:DH + 1]).astype(BF16)


def _fox(h2d, w, layer, b, batch, seq):
    d = h2d.shape[1]
    n = w.shape[2]
    nc = seq // CHUNK
    return pl.pallas_call(
        _fox_kernel,
        grid=(batch,),
        in_specs=[
            pl.BlockSpec((seq, d), lambda i: (i, 0)),
            _layer_spec((d, n), layer),
            _const_spec((1, n)),
        ],
        out_specs=pl.BlockSpec((seq, BRANCH_W), lambda i: (i, 0)),
        out_shape=jax.ShapeDtypeStruct((batch * seq, BRANCH_W), BF16),
        scratch_shapes=[
            pltpu.VMEM((HEADS, nc, 2 * DH, CHUNK), BF16),
            pltpu.VMEM((HEADS, nc, CHUNK, 2 * DH), BF16),
            pltpu.VMEM((SUBLANES, LANES), F32),
            pltpu.VMEM((HEADS, CHUNK, 2 * DH), BF16),
            pltpu.VMEM((HEADS, CHUNK, 2 * DH), F32),
            pltpu.VMEM((CHUNK, n), F32),
        ],
        compiler_params=_params(("arbitrary",)),
        name="forgetting_attention",
    )(h2d, w, b)


def _merge_kernel(h_ref, ym_ref, yl_ref, yr_ref, yf_ref, x_ref, mod_ref, wg_ref, bg_ref,
                  wbr_ref, wo_ref, bo_ref, lng_ref, lnb_ref, o_ref, hn_ref, *, alpha):
    tm, d = x_ref.shape
    for r0 in range(0, tm, tm // MERGE_SPLIT):
        rows = slice(r0, r0 + tm // MERGE_SPLIT)
        h = h_ref[rows, :]
        merged = None
        for n, y_ref in enumerate((ym_ref, yl_ref, yr_ref, yf_ref)):
            gate = jax.nn.sigmoid(
                _dot(h, wg_ref[:, n * d:(n + 1) * d]) + bg_ref[:, n * d:(n + 1) * d])
            term = gate * _dot(y_ref[rows, :], wbr_ref[n])
            merged = term if merged is None else merged + term
        y = _dot(merged.astype(BF16), wo_ref[...]) + bo_ref[...]
        res = alpha * x_ref[rows, :] + (1.0 + mod_ref[2:3, :]) * y
        x_new = _layernorm(res) * lng_ref[...] + lnb_ref[...]
        o_ref[rows, :] = x_new
        hn_ref[rows, :] = _ln_modulated(x_new, mod_ref, shift_row=3, scale_row=4)


def _merge(h2d, ys, x2d, mod_l, wg, layer, bg, wbr, wo, bo, lng, lnb, seq, alpha):
    tokens, d = x2d.shape
    tm = ROW_TILE
    tiles_per_seq = seq // tm
    row = lambda i: (i, 0)
    return pl.pallas_call(
        functools.partial(_merge_kernel, alpha=alpha),
        grid=(tokens // tm,),
        in_specs=[
            pl.BlockSpec((tm, d), row),
            pl.BlockSpec((tm, BRANCH_W), row),
            pl.BlockSpec((tm, BRANCH_W), row),
            pl.BlockSpec((tm, BRANCH_W), row),
            pl.BlockSpec((tm, BRANCH_W), row),
            pl.BlockSpec((tm, d), row),
            pl.BlockSpec((None, 6, d), lambda i: (i // tiles_per_seq, 0, 0)),
            _layer_spec((d, N_BRANCH * d), layer),
            _const_spec((1, N_BRANCH * d)),
            _const_spec((N_BRANCH, BRANCH_W, d)),
            _const_spec((d, d)),
            _const_spec((1, d)),
            _const_spec((1, d)),
            _const_spec((1, d)),
        ],
        out_specs=[pl.BlockSpec((tm, d), row), pl.BlockSpec((tm, d), row)],
        out_shape=[jax.ShapeDtypeStruct((tokens, d), F32),
                   jax.ShapeDtypeStruct((tokens, d), BF16)],
        compiler_params=_params(("parallel",), VMEM_LIMIT_BYTES_WIDE),
        name="merge_out_proj",
    )(h2d, *ys, x2d, mod_l, wg, bg, wbr, wo, bo, lng, lnb)


def _mlp_kernel(h_ref, x_ref, mod_ref, w1_ref, b1_ref, w2_ref, b2_ref, lng_ref, lnb_ref,
                *rest, alpha, ff_tile, emit_next):
    d_ff = w1_ref.shape[1]
    tm = x_ref.shape[0]
    if emit_next:
        next_mod_ref, o_ref, hn_ref = rest
    else:
        (o_ref,) = rest
    for r0 in range(0, tm, tm // MERGE_SPLIT):
        rows = slice(r0, r0 + tm // MERGE_SPLIT)
        h = h_ref[rows, :]
        acc = None
        for c in range(d_ff // ff_tile):
            cols = slice(c * ff_tile, (c + 1) * ff_tile)
            a = jnp.maximum(_dot(h, w1_ref[:, cols]) + b1_ref[:, cols], 0.0)
            part = _dot((a * a).astype(BF16), w2_ref[cols, :])
            acc = part if acc is None else acc + part
        y = acc + b2_ref[...]
        res = alpha * x_ref[rows, :] + (1.0 + mod_ref[5:6, :]) * y
        x_new = _layernorm(res) * lng_ref[...] + lnb_ref[...]
        o_ref[rows, :] = x_new
        if emit_next:
            hn_ref[rows, :] = _ln_modulated(x_new, next_mod_ref, shift_row=0, scale_row=1)


def _mlp(h2d, x2d, mod_l, next_mod, w1, b1, w2, b2, lng, lnb, seq, alpha):
    tokens, d = x2d.shape
    d_ff = w1.shape[1]
    tm = ROW_TILE
    tiles_per_seq = seq // tm
    row = lambda i: (i, 0)
    mod_spec = pl.BlockSpec((None, 6, d), lambda i: (i // tiles_per_seq, 0, 0))
    emit_next = next_mod is not None
    in_specs = [
        pl.BlockSpec((tm, d), row),
        pl.BlockSpec((tm, d), row),
        mod_spec,
        _const_spec((d, d_ff)),
        _const_spec((1, d_ff)),
        _const_spec((d_ff, d)),
        _const_spec((1, d)),
        _const_spec((1, d)),
        _const_spec((1, d)),
    ]
    args = [h2d, x2d, mod_l, w1, b1, w2, b2, lng, lnb]
    out_specs = [pl.BlockSpec((tm, d), row)]
    out_shape = [jax.ShapeDtypeStruct((tokens, d), F32)]
    if emit_next:
        in_specs.append(mod_spec)
        args.append(next_mod)
        out_specs.append(pl.BlockSpec((tm, d), row))
        out_shape.append(jax.ShapeDtypeStruct((tokens, d), BF16))
    outs = pl.pallas_call(
        functools.partial(_mlp_kernel, alpha=alpha, ff_tile=d, emit_next=emit_next),
        grid=(tokens // tm,),
        in_specs=in_specs,
        out_specs=out_specs,
        out_shape=out_shape,
        compiler_params=_params(("parallel",), VMEM_LIMIT_BYTES_WIDE),
        name="sq_relu_mlp",
    )(*args)
    return (outs[0], outs[1]) if emit_next else (outs[0], None)


_PACK_GROUPS = {
    "mlstm": (("m_q", "m_k", "m_v", "m_o"), ("m_i", "m_f")),
    "rglru": (("l_x", "l_g"), ()),
    "retention": (("r_q", "r_k", "r_v", "r_g"), ()),
    "fox": (("f_q", "f_k", "f_v"), ("f_f",)),
    "gate": (("gate",), ()),
}
PACK_COLS = 512


def _pack_kernel(w_ref, o_ref, *, valid):
    depth, d = w_ref.shape[1], w_ref.shape[2]
    col = pl.program_id(0) * PACK_COLS + lax.broadcasted_iota(jnp.int32, (d, PACK_COLS), 1)
    for l in range(depth):
        o_ref[l] = jnp.where(col < valid, w_ref[:, l, :].T, 0.0).astype(BF16)


def _pack_group(w_cols, seg, names, gate_names):
    n_in, depth, d = w_cols.shape
    start = seg[names[0]][0]
    main = seg[names[-1]][1] - start
    assert main % LANES == 0
    gate_w = sum(seg[n][1] - seg[n][0] for n in gate_names)
    assert not gate_names or seg[gate_names[0]][0] == start + main
    width = main + (LANES if gate_names else 0)
    n_blocks = pl.cdiv(width, PACK_COLS)
    assert start + n_blocks * PACK_COLS <= n_in
    return pl.pallas_call(
        functools.partial(_pack_kernel, valid=main + gate_w),
        grid=(n_blocks,),
        in_specs=[pl.BlockSpec((pl.Element(PACK_COLS), pl.Element(depth), pl.Element(d)),
                               lambda j: (start + j * PACK_COLS, 0, 0))],
        out_specs=pl.BlockSpec((depth, d, PACK_COLS), lambda j: (0, 0, j)),
        out_shape=jax.ShapeDtypeStruct((depth, d, width), BF16),
        compiler_params=_params(("parallel",)),
        name="pack_in_proj",
    )(w_cols)


def _pack_weights(w_in, seg):
    w_cols = jnp.transpose(w_in, (2, 0, 1))
    return {k: _pack_group(w_cols, seg, *grp) for k, grp in _PACK_GROUPS.items()}


def _pack_bias(b_in_l, seg, key):
    names, gate_names = _PACK_GROUPS[key]
    bs = [b_in_l[seg[n][0]:seg[n][1]] for n in names]
    if gate_names:
        gb = jnp.concatenate([b_in_l[seg[n][0]:seg[n][1]] for n in gate_names])
        bs.append(jnp.pad(gb, (0, LANES - gb.shape[0])))
    return jnp.concatenate(bs)[None, :]


def _block_diag(w):
    nb, bd, _ = w.shape
    eye = jnp.eye(nb, dtype=w.dtype)
    return (eye[:, None, :, None] * w[:, :, None, :]).reshape(nb * bd, nb * bd)


def kernel(x, c, positions, w_ada, b_ada, w_in, b_in, m_norm, conv_w, conv_b, lru_wa, lru_ba,
           lru_wx, lru_bx, lru_lam, r_norm, w_br, w_out, b_out, ln1_g, ln1_b, w_ff1, b_ff1,
           w_ff2, b_ff2, ln2_g, ln2_b):
    batch, seq, d = x.shape
    depth = w_ada.shape[0]
    assert seq % CHUNK == 0 and seq % ROW_TILE == 0 and d % LANES == 0
    alpha = (2 * depth) ** 0.25
    seg = _segment_slices(d)

    mod = _modulation(c, w_ada, b_ada).reshape(depth, batch, 6, d)
    cos_t, sin_t = _rope_tables(positions)
    x2d = x.reshape(batch * seq, d)
    w = _pack_weights(w_in, seg)
    row = lambda v: v[None, :]
    h1 = _ln_modulate(x2d, mod[0], seq, shift_row=0, scale_row=1)
    for l in range(depth):
        bias = lambda key: _pack_bias(b_in[l], seg, key)
        y_m = _mlstm(h1, w["mlstm"], l, bias("mlstm"), row(m_norm[l]), batch, seq)
        y_l = _rglru(h1, w["rglru"], l, bias("rglru"), conv_w[l], row(conv_b[l]),
                     (0.5 * _block_diag(lru_wa[l])).astype(BF16), row(0.5 * lru_ba[l]),
                     (0.5 * _block_diag(lru_wx[l])).astype(BF16), row(0.5 * lru_bx[l]),
                     row(lru_lam[l]), batch, seq)
        y_r = _retention(h1, w["retention"], l, bias("retention"), cos_t, sin_t,
                         row(r_norm[l]), batch, seq)
        y_f = _fox(h1, w["fox"], l, bias("fox"), batch, seq)
        x2d, h2 = _merge(h1, (y_m, y_l, y_r, y_f), x2d, mod[l], w["gate"], l, bias("gate"),
                         w_br[l].astype(BF16), w_out[l].astype(BF16), row(b_out[l]),
                         row(ln1_g[l]), row(ln1_b[l]), seq, alpha)
        next_mod = mod[l + 1] if l + 1 < depth else None
        x2d, h1 = _mlp(h2, x2d, mod[l], next_mod, w_ff1[l].astype(BF16), row(b_ff1[l]),
                       w_ff2[l].astype(BF16), row(b_ff2[l]), row(ln2_g[l]), row(ln2_b[l]),
                       seq, alpha)
    return x2d.reshape(batch, seq, d)
```

```python
import functools
import math

import jax
import jax.numpy as jnp
from jax import lax
from jax.experimental import pallas as pl
from jax.experimental.pallas import tpu as pltpu

F32 = jnp.float32
BF16 = jnp.bfloat16

LN_EPS = 1e-5
HEADS = 4
DH = 128
BRANCH_W = HEADS * DH
LRU_BLOCKS = 8
LRU_C = 8.0
CONV_W = 4
ROPE_BASE = 10000.0
N_BRANCH = 4

LANES = 128
SUBLANES = 8
CHUNK = 256
ROW_TILE = 1024
MERGE_SPLIT = 4
VMEM_LIMIT_BYTES = 48 * 1024 * 1024
VMEM_LIMIT_BYTES_WIDE = 58 * 1024 * 1024

_SEGMENTS = (
    ("m_q", BRANCH_W), ("m_k", BRANCH_W), ("m_v", BRANCH_W), ("m_o", BRANCH_W),
    ("m_i", HEADS), ("m_f", HEADS),
    ("l_x", BRANCH_W), ("l_g", BRANCH_W),
    ("r_q", BRANCH_W), ("r_k", BRANCH_W), ("r_v", BRANCH_W), ("r_g", BRANCH_W),
    ("f_q", BRANCH_W), ("f_k", BRANCH_W), ("f_v", BRANCH_W), ("f_f", HEADS),
    ("gate", None),
)


def _segment_slices(d_model):
    out, acc = {}, 0
    for name, width in _SEGMENTS:
        width = N_BRANCH * d_model if width is None else width
        out[name] = (acc, acc + width)
        acc += width
    return out


def _dot(a, b):
    return jnp.dot(a, b, preferred_element_type=F32)


def _dot_f32(a, b):
    return jnp.dot(a, b, preferred_element_type=F32, precision=lax.Precision.HIGHEST)


def _layernorm(x):
    mu = jnp.mean(x, axis=-1, keepdims=True)
    xc = x - mu
    var = jnp.mean(xc * xc, axis=-1, keepdims=True)
    return xc * lax.rsqrt(var + LN_EPS)


def _log_sigmoid(x):
    return -(jnp.maximum(-x, 0.0) + jnp.log1p(jnp.exp(-jnp.abs(x))))


def _softplus(x):
    return jnp.maximum(x, 0.0) + jnp.log1p(jnp.exp(-jnp.abs(x)))


def _gelu_tanh(x):
    return x * (0.5 * (1.0 + jnp.tanh(math.sqrt(2.0 / math.pi) * (x + 0.044715 * (x * x * x)))))


def _causal_mask(n):
    row = lax.broadcasted_iota(jnp.int32, (n, n), 0)
    col = lax.broadcasted_iota(jnp.int32, (n, n), 1)
    return row >= col


def _prefix_sum_matrix(n, dtype):
    row = lax.broadcasted_iota(jnp.int32, (n, n), 0)
    col = lax.broadcasted_iota(jnp.int32, (n, n), 1)
    return jnp.where(row <= col, 1.0, 0.0).astype(dtype)


def _bf16_pieces(x):
    hi = x.astype(BF16).astype(F32)
    mid = (x - hi).astype(BF16).astype(F32)
    lo = (x - hi - mid).astype(BF16).astype(F32)
    return hi, mid, lo


def _prefix_sum_lanes(x, tri):
    rows = x.shape[0]
    stacked = jnp.concatenate(_bf16_pieces(x) + (jnp.zeros_like(x),), axis=0).astype(BF16)
    y = _dot(stacked, tri)
    return y[0:rows] + y[rows:2 * rows] + y[2 * rows:3 * rows]


def _ones_column_block(n):
    lane = lax.broadcasted_iota(jnp.int32, (n, LANES), 1)
    return jnp.where(lane == 0, 1.0, 0.0).astype(F32)


def _head_norm(y, gain):
    mu = jnp.mean(y, axis=-1, keepdims=True)
    yc = y - mu
    var = jnp.mean(yc * yc, axis=-1, keepdims=True)
    return yc * lax.rsqrt(var + LN_EPS) * gain


def _const_spec(shape):
    zeros = (0,) * len(shape)
    return pl.BlockSpec(shape, lambda *_: zeros, pipeline_mode=pl.Buffered(1))


def _layer_spec(shape, layer):
    idx = (layer,) + (0,) * len(shape)
    return pl.BlockSpec((None,) + tuple(shape), lambda *_: idx, pipeline_mode=pl.Buffered(1))


def _params(semantics, vmem_limit=VMEM_LIMIT_BYTES):
    return pltpu.CompilerParams(dimension_semantics=semantics, vmem_limit_bytes=vmem_limit)


def _mod_kernel(c_ref, w_ref, b_ref, o_ref):
    c = c_ref[...]
    cond = c * jax.nn.sigmoid(c)
    o_ref[...] = _dot_f32(cond, w_ref[...]) + b_ref[...]


def _modulation(c, w_ada, b_ada):
    depth, d, six_d = w_ada.shape
    batch = c.shape[0]
    n_tiles = six_d // d
    return pl.pallas_call(
        _mod_kernel,
        grid=(depth, n_tiles),
        in_specs=[
            pl.BlockSpec((batch, d), lambda l, j: (0, 0)),
            pl.BlockSpec((None, d, d), lambda l, j: (l, 0, j)),
            pl.BlockSpec((None, 1, d), lambda l, j: (l, 0, j)),
        ],
        out_specs=pl.BlockSpec((None, batch, d), lambda l, j: (l, 0, j)),
        out_shape=jax.ShapeDtypeStruct((depth, batch, six_d), F32),
        compiler_params=_params(("parallel", "parallel")),
        name="adaln_modulation",
    )(c, w_ada, b_ada.reshape(depth, 1, six_d))


def _rope_kernel(pos_ref, invf_ref, cos_ref, sin_ref):
    ang = invf_ref[...] * pos_ref[...].astype(F32)
    c = jnp.cos(ang)
    s = jnp.sin(ang)
    cos_ref[...] = jnp.concatenate([c, c], axis=0).T
    sin_ref[...] = jnp.concatenate([-s, s], axis=0).T


def _rope_tables(positions):
    batch, seq = positions.shape
    inv_freq = ROPE_BASE ** (-jnp.arange(0, DH, 2, dtype=F32) / DH)
    out = jax.ShapeDtypeStruct((batch, seq, DH), F32)
    return pl.pallas_call(
        _rope_kernel,
        grid=(batch,),
        in_specs=[
            pl.BlockSpec((None, 1, seq), lambda b: (b, 0, 0)),
            pl.BlockSpec((DH // 2, 1), lambda b: (0, 0)),
        ],
        out_specs=[pl.BlockSpec((None, seq, DH), lambda b: (b, 0, 0))] * 2,
        out_shape=[out, out],
        compiler_params=_params(("parallel",)),
        name="rope_tables",
    )(positions.reshape(batch, 1, seq), inv_freq.reshape(DH // 2, 1))


def _ln_modulated(x, mod_ref, shift_row, scale_row):
    scale = mod_ref[scale_row:scale_row + 1, :]
    shift = mod_ref[shift_row:shift_row + 1, :]
    return (_layernorm(x) * (1.0 + scale) + shift).astype(BF16)


def _ln_mod_kernel(x_ref, mod_ref, o_ref, *, shift_row, scale_row):
    o_ref[...] = _ln_modulated(x_ref[...], mod_ref, shift_row, scale_row)


def _ln_modulate(x2d, mod_l, seq, shift_row, scale_row):
    tokens, d = x2d.shape
    tm = ROW_TILE
    tiles_per_seq = seq // tm
    return pl.pallas_call(
        functools.partial(_ln_mod_kernel, shift_row=shift_row, scale_row=scale_row),
        grid=(tokens // tm,),
        in_specs=[
            pl.BlockSpec((tm, d), lambda i: (i, 0)),
            pl.BlockSpec((None, 6, d), lambda i: (i // tiles_per_seq, 0, 0)),
        ],
        out_specs=pl.BlockSpec((tm, d), lambda i: (i, 0)),
        out_shape=jax.ShapeDtypeStruct((tokens, d), BF16),
        compiler_params=_params(("parallel",)),
        name="ln_modulate",
    )(x2d, mod_l)


def _mlstm_kernel(h_ref, w_ref, b_ref, gain_ref, o_ref, state_ref, mc_ref, bc_ref, z_ref):
    seq = h_ref.shape[0]
    L = CHUNK
    nc = seq // L
    qkvo = HEADS * DH
    state_ref[...] = jnp.zeros_like(state_ref)
    mc_ref[...] = jnp.zeros_like(mc_ref)
    bc_ref[...] = jnp.zeros_like(bc_ref)
    tri = _prefix_sum_matrix(L, BF16)

    def project(i):
        r0 = pl.multiple_of(i * L, L)
        z_ref[...] = _dot(h_ref[pl.ds(r0, L), :], w_ref[...]) + b_ref[...]

    def chunk(i, project_next):
        r0 = pl.multiple_of(i * L, L)
        z = z_ref[...]
        causal = _causal_mask(L)
        gates = z[:, 4 * qkvo:4 * qkvo + LANES].T[0:2 * HEADS, :]
        log_f = _log_sigmoid(gates)
        cum_f = _prefix_sum_lanes(log_f, tri)
        ones_blk = _ones_column_block(L)
        heads = []
        for hd in range(HEADS):
            lo = hd * DH
            heads.append((
                z[:, lo:lo + DH].astype(BF16),
                (z[:, qkvo + lo:qkvo + lo + DH] * DH ** -0.5).T,
                jnp.concatenate(
                    [z[:, 2 * qkvo + lo:2 * qkvo + lo + DH], ones_blk], axis=1).astype(BF16),
                jax.nn.sigmoid(z[:, 3 * qkvo + lo:3 * qkvo + lo + DH])))
        if project_next:
            project(i + 1)
        for hd, (q, k_t, v_aug, o_gate) in enumerate(heads):
            lo = hd * DH
            mc = mc_ref[hd][:, 0:1]
            bc = bc_ref[hd][:, 0:1]
            lf_row = log_f[HEADS + hd:HEADS + hd + 1, :]
            b_row = cum_f[HEADS + hd:HEADS + hd + 1, :] + bc
            c_row = gates[hd:hd + 1, :] - b_row
            d0 = jnp.where(causal, c_row, -jnp.inf)
            m_t = jnp.maximum(jnp.max(d0, axis=1, keepdims=True), mc)
            p = jnp.exp(d0 - m_t)
            b_col = jnp.sum(jnp.where(causal, lf_row, 0.0), axis=1, keepdims=True) + bc
            s = _dot(q, k_t.astype(BF16))
            intra = _dot((s * p).astype(BF16), v_aug)
            st = state_ref[hd]
            inter = _dot(q, st.astype(BF16))
            tot = jnp.exp(mc - m_t) * inter + intra
            den = tot[:, DH:DH + 1]
            hh = tot[:, :DH] / jnp.maximum(jnp.abs(den), jnp.exp(-(b_col + m_t)))
            y = _head_norm(o_gate * hh, gain_ref[:, lo:lo + DH])
            o_ref[pl.ds(r0, L), lo:lo + DH] = y.astype(BF16)
            mc_new = m_t[L - 1:L, :]
            k_w = (k_t * jnp.exp(c_row - mc_new)).astype(BF16)
            state_ref[hd] = jnp.exp(mc - mc_new) * st + _dot(k_w, v_aug)
            mc_ref[hd] = jnp.broadcast_to(mc_new, (1, LANES))
            bc_ref[hd] = jnp.broadcast_to(b_row[:, L - 1:L], (1, LANES))

    project(0)
    for i in range(nc):
        chunk(i, project_next=i + 1 < nc)


def _mlstm(h2d, w, layer, b, gain, batch, seq):
    d = h2d.shape[1]
    n = w.shape[2]
    return pl.pallas_call(
        _mlstm_kernel,
        grid=(batch,),
        in_specs=[
            pl.BlockSpec((seq, d), lambda i: (i, 0)),
            _layer_spec((d, n), layer),
            _const_spec((1, n)),
            _const_spec((1, BRANCH_W)),
        ],
        out_specs=pl.BlockSpec((seq, BRANCH_W), lambda i: (i, 0)),
        out_shape=jax.ShapeDtypeStruct((batch * seq, BRANCH_W), BF16),
        scratch_shapes=[
            pltpu.VMEM((HEADS, DH, 2 * DH), F32),
            pltpu.VMEM((HEADS, 1, LANES), F32),
            pltpu.VMEM((HEADS, 1, LANES), F32),
            pltpu.VMEM((CHUNK, n), F32),
        ],
        compiler_params=_params(("arbitrary",)),
        name="mlstm",
    )(h2d, w, b, gain)


def _rglru_kernel(h_ref, w_ref, b_ref, cw_ref, cb_ref, wa_ref, ba_ref, wx_ref, bx_ref,
                  lam_ref, o_ref, xpad_ref, hbuf_ref, hcar_ref):
    seq = h_ref.shape[0]
    L = CHUNK
    R = BRANCH_W
    pad = SUBLANES
    xpad_ref[0:pad, :] = jnp.zeros((pad, R), F32)
    hcar_ref[...] = jnp.zeros_like(hcar_ref)

    groups = L // SUBLANES

    def step(i, carry):
        r0 = pl.multiple_of(i * L, L)
        z = _dot(h_ref[pl.ds(r0, L), :], w_ref[...]) + b_ref[...]
        xpad_ref[pad:pad + L, :] = z[:, :R]
        xp = xpad_ref[...].reshape(groups + 1, SUBLANES, R)
        xpad_ref[0:pad, :] = xp[groups]
        sub = lax.broadcasted_iota(jnp.int32, (groups, SUBLANES, R), 1)
        xc = cb_ref[...] + cw_ref[CONV_W - 1:CONV_W, :] * xp[1:]
        for k in range(1, CONV_W):
            rot = pltpu.roll(xp, k, 1)
            xc = xc + cw_ref[CONV_W - 1 - k:CONV_W - k, :] * jnp.where(sub >= k, rot[1:], rot[:-1])
        xc = xc.reshape(L, R)
        xcb = xc.astype(BF16)
        r2 = jnp.tanh(_dot(xcb, wa_ref[...]) + ba_ref[...]) + 1.0
        i2 = jnp.tanh(_dot(xcb, wx_ref[...]) + bx_ref[...]) + 1.0
        log_a = r2 * ((-0.5 * LRU_C) * _softplus(-lam_ref[...]))
        a = jnp.exp(log_a)
        th = jnp.tanh(log_a)
        u = jnp.exp(0.5 * jnp.log(-2.0 * th / (1.0 - th))) * (i2 * (0.5 * xc))
        a = a.reshape(groups, SUBLANES, R)
        u = u.reshape(groups, SUBLANES, R)
        for sft in (1, 2, 4):
            ok = sub >= sft
            u = jnp.where(ok, a * pltpu.roll(u, sft, 1) + u, u)
            a = jnp.where(ok, a * pltpu.roll(a, sft, 1), a)
        hprev = hcar_ref[...]
        for g in range(groups):
            rows = slice(g * SUBLANES, (g + 1) * SUBLANES)
            hg = a[g] * hprev + u[g]
            hbuf_ref[rows, :] = hg
            hprev = hg[SUBLANES - 1:SUBLANES, :]
        hcar_ref[...] = hprev
        o_ref[pl.ds(r0, L), :] = (hbuf_ref[...] * _gelu_tanh(z[:, R:2 * R])).astype(BF16)
        return carry

    lax.fori_loop(0, seq // L, step, 0)


def _rglru(h2d, w, layer, b, conv_w, conv_b, wa, ba, wx, bx, lam, batch, seq):
    d = h2d.shape[1]
    R = BRANCH_W
    return pl.pallas_call(
        _rglru_kernel,
        grid=(batch,),
        in_specs=[
            pl.BlockSpec((seq, d), lambda i: (i, 0)),
            _layer_spec((d, 2 * R), layer),
            _const_spec((1, 2 * R)),
            _const_spec((CONV_W, R)),
            _const_spec((1, R)),
            _const_spec((R, R)),
            _const_spec((1, R)),
            _const_spec((R, R)),
            _const_spec((1, R)),
            _const_spec((1, R)),
        ],
        out_specs=pl.BlockSpec((seq, R), lambda i: (i, 0)),
        out_shape=jax.ShapeDtypeStruct((batch * seq, R), BF16),
        scratch_shapes=[
            pltpu.VMEM((CHUNK + SUBLANES, R), F32),
            pltpu.VMEM((CHUNK, R), F32),
            pltpu.VMEM((1, R), F32),
        ],
        compiler_params=_params(("arbitrary",)),
        name="rglru",
    )(h2d, w, b, conv_w, conv_b, wa, ba, wx, bx, lam)


def _log_gamma(hd):
    return math.log1p(-(2.0 ** (-5.0 - hd)))


def _retention_kernel(h_ref, w_ref, b_ref, cos_ref, sin_ref, gain_ref, o_ref, state_ref,
                      decay_ref, z_ref):
    seq = h_ref.shape[0]
    L = CHUNK
    nc = seq // L
    W = HEADS * DH
    state_ref[...] = jnp.zeros_like(state_ref)
    row = lax.broadcasted_iota(jnp.int32, (L, L), 0)
    col = lax.broadcasted_iota(jnp.int32, (L, L), 1)
    rel = (row - col).astype(F32)
    for hd in range(HEADS):
        decay_ref[hd] = jnp.where(rel >= 0.0, jnp.exp(rel * _log_gamma(hd)), 0.0)

    def project(i):
        r0 = pl.multiple_of(i * L, L)
        z_ref[...] = _dot(h_ref[pl.ds(r0, L), :], w_ref[...]) + b_ref[...]

    def chunk(i, project_next):
        r0 = pl.multiple_of(i * L, L)
        z = z_ref[...]
        cosf = cos_ref[pl.ds(r0, L), :]
        sinf = sin_ref[pl.ds(r0, L), :]
        idx_col = lax.broadcasted_iota(jnp.int32, (L, 1), 0).astype(F32)
        idx_row = lax.broadcasted_iota(jnp.int32, (1, L), 1).astype(F32)
        heads = []
        for hd in range(HEADS):
            lo = hd * DH
            qf = z[:, lo:lo + DH]
            kf = z[:, W + lo:W + lo + DH]
            heads.append((
                (qf * cosf + pltpu.roll(qf, DH // 2, 1) * sinf).astype(BF16),
                ((kf * cosf + pltpu.roll(kf, DH // 2, 1) * sinf) * DH ** -0.5).T,
                z[:, 2 * W + lo:2 * W + lo + DH].astype(BF16),
                z[:, 3 * W + lo:3 * W + lo + DH]))
        if project_next:
            project(i + 1)
        for hd, (q, k_t, v, g) in enumerate(heads):
            lo = hd * DH
            lg = _log_gamma(hd)
            s = _dot(q, k_t.astype(BF16)) * decay_ref[hd]
            st = state_ref[hd]
            out = _dot(s.astype(BF16), v) + jnp.exp((idx_col + 1.0) * lg) * _dot(q, st.astype(BF16))
            k_w = (k_t * jnp.exp((L - 1.0 - idx_row) * lg)).astype(BF16)
            state_ref[hd] = math.exp(L * lg) * st + _dot(k_w, v)
            y = (g * jax.nn.sigmoid(g)) * _head_norm(out, gain_ref[:, lo:lo + DH])
            o_ref[pl.ds(r0, L), lo:lo + DH] = y.astype(BF16)

    project(0)
    for i in range(nc):
        chunk(i, project_next=i + 1 < nc)


def _retention(h2d, w, layer, b, cos_t, sin_t, gain, batch, seq):
    d = h2d.shape[1]
    n = w.shape[2]
    return pl.pallas_call(
        _retention_kernel,
        grid=(batch,),
        in_specs=[
            pl.BlockSpec((seq, d), lambda i: (i, 0)),
            _layer_spec((d, n), layer),
            _const_spec((1, n)),
            pl.BlockSpec((None, seq, DH), lambda i: (i, 0, 0)),
            pl.BlockSpec((None, seq, DH), lambda i: (i, 0, 0)),
            _const_spec((1, BRANCH_W)),
        ],
        out_specs=pl.BlockSpec((seq, BRANCH_W), lambda i: (i, 0)),
        out_shape=jax.ShapeDtypeStruct((batch * seq, BRANCH_W), BF16),
        scratch_shapes=[
            pltpu.VMEM((HEADS, DH, DH), F32),
            pltpu.VMEM((HEADS, CHUNK, CHUNK), F32),
            pltpu.VMEM((CHUNK, n), F32),
        ],
        compiler_params=_params(("arbitrary",)),
        name="retention",
    )(h2d, w, b, cos_t, sin_t, gain)


def _fox_kernel(h_ref, w_ref, b_ref, o_ref, kt_ref, v_ref, fc_ref, q_ref, acc_ref):
    seq = h_ref.shape[0]
    L = CHUNK
    W = HEADS * DH
    fc_ref[...] = jnp.zeros_like(fc_ref)
    tri = _prefix_sum_matrix(L, BF16)
    causal = _causal_mask(L)
    sub = lax.broadcasted_iota(jnp.int32, (SUBLANES, L), 0)
    lane = lax.broadcasted_iota(jnp.int32, (L, LANES), 1)
    q_tail = jnp.where(lane < 3, 1.0, 0.0).astype(BF16)
    k_pad = jnp.zeros((DH - SUBLANES, L), F32)
    ones_blk = _ones_column_block(L)

    for i in range(seq // L):
        r0 = i * L
        z = _dot(h_ref[r0:r0 + L, :], w_ref[...]) + b_ref[...]
        f_pre = z[:, 3 * W:3 * W + LANES].T[0:SUBLANES, :]
        cum_f = _prefix_sum_lanes(_log_sigmoid(f_pre), tri) + fc_ref[:, 0:1]
        fc_ref[...] = jnp.broadcast_to(cum_f[:, L - 1:L], (SUBLANES, LANES))
        f_pieces = _bf16_pieces(cum_f)
        row_max = []
        for hd in range(HEADS):
            lo = hd * DH
            p0, p1, p2 = (jnp.broadcast_to(-f[hd:hd + 1, :], (SUBLANES, L)) for f in f_pieces)
            bias_rows = jnp.where(sub == 0, p0, jnp.where(sub == 1, p1, jnp.where(sub == 2, p2, 0.0)))
            k_t = jnp.concatenate(
                [z[:, W + lo:W + lo + DH].T, bias_rows, k_pad], axis=0).astype(BF16)
            v_aug = jnp.concatenate(
                [z[:, 2 * W + lo:2 * W + lo + DH], ones_blk], axis=1).astype(BF16)
            kt_ref[hd, i] = k_t
            v_ref[hd, i] = v_aug
            q = jnp.concatenate(
                [(z[:, lo:lo + DH] * DH ** -0.5).astype(BF16), q_tail], axis=1)
            q_ref[hd] = q
            s = jnp.where(causal, _dot(q, k_t), -jnp.inf)
            m = jnp.max(s, axis=1, keepdims=True)
            acc_ref[hd] = _dot(jnp.exp(s - m).astype(BF16), v_aug)
            row_max.append(m)

        for j in range(i):
            for hd in range(HEADS):
                s = _dot(q_ref[hd], kt_ref[hd, j])
                m_new = jnp.maximum(row_max[hd], jnp.max(s, axis=1, keepdims=True))
                acc_ref[hd] = (jnp.exp(row_max[hd] - m_new) * acc_ref[hd]
                               + _dot(jnp.exp(s - m_new).astype(BF16), v_ref[hd, j]))
                row_max[hd] = m_new
        for hd in range(HEADS):
            acc = acc_ref[hd]
            o_ref[r0:r0 + L, hd * DH:(hd + 1) * DH] = (
                acc[:, :DH] / acc[:, DH:DH + 1]).astype(BF16)


def _fox(h2d, w, layer, b, batch, seq):
    d = h2d.shape[1]
    n = w.shape[2]
    nc = seq // CHUNK
    return pl.pallas_call(
        _fox_kernel,
        grid=(batch,),
        in_specs=[
            pl.BlockSpec((seq, d), lambda i: (i, 0)),
            _layer_spec((d, n), layer),
            _const_spec((1, n)),
        ],
        out_specs=pl.BlockSpec((seq, BRANCH_W), lambda i: (i, 0)),
        out_shape=jax.ShapeDtypeStruct((batch * seq, BRANCH_W), BF16),
        scratch_shapes=[
            pltpu.VMEM((HEADS, nc, 2 * DH, CHUNK), BF16),
            pltpu.VMEM((HEADS, nc, CHUNK, 2 * DH), BF16),
            pltpu.VMEM((SUBLANES, LANES), F32),
            pltpu.VMEM((HEADS, CHUNK, 2 * DH), BF16),
            pltpu.VMEM((HEADS, CHUNK, 2 * DH), F32),
        ],
        compiler_params=_params(("arbitrary",)),
        name="forgetting_attention",
    )(h2d, w, b)


def _merge_kernel(h_ref, ym_ref, yl_ref, yr_ref, yf_ref, x_ref, mod_ref, wg_ref, bg_ref,
                  wbr_ref, wo_ref, bo_ref, lng_ref, lnb_ref, o_ref, hn_ref, *, alpha):
    tm, d = x_ref.shape
    for r0 in range(0, tm, tm // MERGE_SPLIT):
        rows = slice(r0, r0 + tm // MERGE_SPLIT)
        h = h_ref[rows, :]
        merged = None
        for n, y_ref in enumerate((ym_ref, yl_ref, yr_ref, yf_ref)):
            gate = jax.nn.sigmoid(
                _dot(h, wg_ref[:, n * d:(n + 1) * d]) + bg_ref[:, n * d:(n + 1) * d])
            term = gate * _dot(y_ref[rows, :], wbr_ref[n])
            merged = term if merged is None else merged + term
        y = _dot(merged.astype(BF16), wo_ref[...]) + bo_ref[...]
        res = alpha * x_ref[rows, :] + (1.0 + mod_ref[2:3, :]) * y
        x_new = _layernorm(res) * lng_ref[...] + lnb_ref[...]
        o_ref[rows, :] = x_new
        hn_ref[rows, :] = _ln_modulated(x_new, mod_ref, shift_row=3, scale_row=4)


def _merge(h2d, ys, x2d, mod_l, wg, layer, bg, wbr, wo, bo, lng, lnb, seq, alpha):
    tokens, d = x2d.shape
    tm = ROW_TILE
    tiles_per_seq = seq // tm
    row = lambda i: (i, 0)
    return pl.pallas_call(
        functools.partial(_merge_kernel, alpha=alpha),
        grid=(tokens // tm,),
        in_specs=[
            pl.BlockSpec((tm, d), row),
            pl.BlockSpec((tm, BRANCH_W), row),
            pl.BlockSpec((tm, BRANCH_W), row),
            pl.BlockSpec((tm, BRANCH_W), row),
            pl.BlockSpec((tm, BRANCH_W), row),
            pl.BlockSpec((tm, d), row),
            pl.BlockSpec((None, 6, d), lambda i: (i // tiles_per_seq, 0, 0)),
            _layer_spec((d, N_BRANCH * d), layer),
            _const_spec((1, N_BRANCH * d)),
            _const_spec((N_BRANCH, BRANCH_W, d)),
            _const_spec((d, d)),
            _const_spec((1, d)),
            _const_spec((1, d)),
            _const_spec((1, d)),
        ],
        out_specs=[pl.BlockSpec((tm, d), row), pl.BlockSpec((tm, d), row)],
        out_shape=[jax.ShapeDtypeStruct((tokens, d), F32),
                   jax.ShapeDtypeStruct((tokens, d), BF16)],
        compiler_params=_params(("parallel",), VMEM_LIMIT_BYTES_WIDE),
        name="merge_out_proj",
    )(h2d, *ys, x2d, mod_l, wg, bg, wbr, wo, bo, lng, lnb)


def _mlp_kernel(h_ref, x_ref, mod_ref, w1_ref, b1_ref, w2_ref, b2_ref, lng_ref, lnb_ref,
                *rest, alpha, ff_tile, emit_next):
    d_ff = w1_ref.shape[1]
    tm = x_ref.shape[0]
    if emit_next:
        next_mod_ref, o_ref, hn_ref = rest
    else:
        (o_ref,) = rest
    for r0 in range(0, tm, tm // MERGE_SPLIT):
        rows = slice(r0, r0 + tm // MERGE_SPLIT)
        h = h_ref[rows, :]
        acc = None
        for c in range(d_ff // ff_tile):
            cols = slice(c * ff_tile, (c + 1) * ff_tile)
            a = jnp.maximum(_dot(h, w1_ref[:, cols]) + b1_ref[:, cols], 0.0)
            part = _dot((a * a).astype(BF16), w2_ref[cols, :])
            acc = part if acc is None else acc + part
        y = acc + b2_ref[...]
        res = alpha * x_ref[rows, :] + (1.0 + mod_ref[5:6, :]) * y
        x_new = _layernorm(res) * lng_ref[...] + lnb_ref[...]
        o_ref[rows, :] = x_new
        if emit_next:
            hn_ref[rows, :] = _ln_modulated(x_new, next_mod_ref, shift_row=0, scale_row=1)


def _mlp(h2d, x2d, mod_l, next_mod, w1, b1, w2, b2, lng, lnb, seq, alpha):
    tokens, d = x2d.shape
    d_ff = w1.shape[1]
    tm = ROW_TILE
    tiles_per_seq = seq // tm
    row = lambda i: (i, 0)
    mod_spec = pl.BlockSpec((None, 6, d), lambda i: (i // tiles_per_seq, 0, 0))
    emit_next = next_mod is not None
    in_specs = [
        pl.BlockSpec((tm, d), row),
        pl.BlockSpec((tm, d), row),
        mod_spec,
        _const_spec((d, d_ff)),
        _const_spec((1, d_ff)),
        _const_spec((d_ff, d)),
        _const_spec((1, d)),
        _const_spec((1, d)),
        _const_spec((1, d)),
    ]
    args = [h2d, x2d, mod_l, w1, b1, w2, b2, lng, lnb]
    out_specs = [pl.BlockSpec((tm, d), row)]
    out_shape = [jax.ShapeDtypeStruct((tokens, d), F32)]
    if emit_next:
        in_specs.append(mod_spec)
        args.append(next_mod)
        out_specs.append(pl.BlockSpec((tm, d), row))
        out_shape.append(jax.ShapeDtypeStruct((tokens, d), BF16))
    outs = pl.pallas_call(
        functools.partial(_mlp_kernel, alpha=alpha, ff_tile=d, emit_next=emit_next),
        grid=(tokens // tm,),
        in_specs=in_specs,
        out_specs=out_specs,
        out_shape=out_shape,
        compiler_params=_params(("parallel",), VMEM_LIMIT_BYTES_WIDE),
        name="sq_relu_mlp",
    )(*args)
    return (outs[0], outs[1]) if emit_next else (outs[0], None)


_PACK_GROUPS = {
    "mlstm": (("m_q", "m_k", "m_v", "m_o"), ("m_i", "m_f")),
    "rglru": (("l_x", "l_g"), ()),
    "retention": (("r_q", "r_k", "r_v", "r_g"), ()),
    "fox": (("f_q", "f_k", "f_v"), ("f_f",)),
    "gate": (("gate",), ()),
}
PACK_COLS = 512


def _pack_kernel(w_ref, o_ref, *, valid):
    depth, d = w_ref.shape[1], w_ref.shape[2]
    col = pl.program_id(0) * PACK_COLS + lax.broadcasted_iota(jnp.int32, (d, PACK_COLS), 1)
    for l in range(depth):
        o_ref[l] = jnp.where(col < valid, w_ref[:, l, :].T, 0.0).astype(BF16)


def _pack_group(w_cols, seg, names, gate_names):
    n_in, depth, d = w_cols.shape
    start = seg[names[0]][0]
    main = seg[names[-1]][1] - start
    assert main % LANES == 0
    gate_w = sum(seg[n][1] - seg[n][0] for n in gate_names)
    assert not gate_names or seg[gate_names[0]][0] == start + main
    width = main + (LANES if gate_names else 0)
    n_blocks = pl.cdiv(width, PACK_COLS)
    assert start + n_blocks * PACK_COLS <= n_in
    return pl.pallas_call(
        functools.partial(_pack_kernel, valid=main + gate_w),
        grid=(n_blocks,),
        in_specs=[pl.BlockSpec((pl.Element(PACK_COLS), pl.Element(depth), pl.Element(d)),
                               lambda j: (start + j * PACK_COLS, 0, 0))],
        out_specs=pl.BlockSpec((depth, d, PACK_COLS), lambda j: (0, 0, j)),
        out_shape=jax.ShapeDtypeStruct((depth, d, width), BF16),
        compiler_params=_params(("parallel",)),
        name="pack_in_proj",
    )(w_cols)


def _pack_weights(w_in, seg):
    w_cols = jnp.transpose(w_in, (2, 0, 1))
    return {k: _pack_group(w_cols, seg, *grp) for k, grp in _PACK_GROUPS.items()}


def _pack_bias(b_in_l, seg, key):
    names, gate_names = _PACK_GROUPS[key]
    bs = [b_in_l[seg[n][0]:seg[n][1]] for n in names]
    if gate_names:
        gb = jnp.concatenate([b_in_l[seg[n][0]:seg[n][1]] for n in gate_names])
        bs.append(jnp.pad(gb, (0, LANES - gb.shape[0])))
    return jnp.concatenate(bs)[None, :]


def _block_diag(w):
    nb, bd, _ = w.shape
    eye = jnp.eye(nb, dtype=w.dtype)
    return (eye[:, None, :, None] * w[:, :, None, :]).reshape(nb * bd, nb * bd)


def kernel(x, c, positions, w_ada, b_ada, w_in, b_in, m_norm, conv_w, conv_b, lru_wa, lru_ba,
           lru_wx, lru_bx, lru_lam, r_norm, w_br, w_out, b_out, ln1_g, ln1_b, w_ff1, b_ff1,
           w_ff2, b_ff2, ln2_g, ln2_b):
    batch, seq, d = x.shape
    depth = w_ada.shape[0]
    assert seq % CHUNK == 0 and seq % ROW_TILE == 0 and d % LANES == 0
    alpha = (2 * depth) ** 0.25
    seg = _segment_slices(d)

    mod = _modulation(c, w_ada, b_ada).reshape(depth, batch, 6, d)
    cos_t, sin_t = _rope_tables(positions)
    x2d = x.reshape(batch * seq, d)
    w = _pack_weights(w_in, seg)
    row = lambda v: v[None, :]
    h1 = _ln_modulate(x2d, mod[0], seq, shift_row=0, scale_row=1)
    for l in range(depth):
        bias = lambda key: _pack_bias(b_in[l], seg, key)
        y_m = _mlstm(h1, w["mlstm"], l, bias("mlstm"), row(m_norm[l]), batch, seq)
        y_l = _rglru(h1, w["rglru"], l, bias("rglru"), conv_w[l], row(conv_b[l]),
                     (0.5 * _block_diag(lru_wa[l])).astype(BF16), row(0.5 * lru_ba[l]),
                     (0.5 * _block_diag(lru_wx[l])).astype(BF16), row(0.5 * lru_bx[l]),
                     row(lru_lam[l]), batch, seq)
        y_r = _retention(h1, w["retention"], l, bias("retention"), cos_t, sin_t,
                         row(r_norm[l]), batch, seq)
        y_f = _fox(h1, w["fox"], l, bias("fox"), batch, seq)
        x2d, h2 = _merge(h1, (y_m, y_l, y_r, y_f), x2d, mod[l], w["gate"], l, bias("gate"),
                         w_br[l].astype(BF16), w_out[l].astype(BF16), row(b_out[l]),
                         row(ln1_g[l]), row(ln1_b[l]), seq, alpha)
        next_mod = mod[l + 1] if l + 1 < depth else None
        x2d, h1 = _mlp(h2, x2d, mod[l], next_mod, w_ff1[l].astype(BF16), row(b_ff1[l]),
                       w_ff2[l].astype(BF16), row(b_ff2[l]), row(ln2_g[l]), row(ln2_b[l]),
                       seq, alpha)
    return x2d.reshape(batch, seq, d)
```
